```python
import jax, jax.numpy as jnp
from jax import lax
import numpy as np

D_MODEL = 2048
BATCH = 32
SEQ = 256
DEPTH = 1
DEC_BATCH = 4
DEC_SEQ = 2048
PAST_LEN = 256

GRID_W = 64
HEAD_DIM = 128
AXIS_DIM = HEAD_DIM // 2
N_HEADS_GLOB = 8
N_KV_GLOB = 2
G_GLOB = N_HEADS_GLOB // N_KV_GLOB
N_HEADS_WIN = 8
N_KV_WIN = 2
G_WIN = N_HEADS_WIN // N_KV_WIN
MIX_WIDTH = (N_HEADS_GLOB + N_HEADS_WIN) * HEAD_DIM
QG_W = N_HEADS_GLOB * HEAD_DIM
KG_W = N_KV_GLOB * HEAD_DIM
QW_W = N_HEADS_WIN * HEAD_DIM
KW_W = N_KV_WIN * HEAD_DIM
QKV_WIDTH = QG_W + 2 * KG_W + QW_W + 2 * KW_W
QKV_SPLITS = (QG_W, QG_W + KG_W, QG_W + 2 * KG_W, QG_W + 2 * KG_W + QW_W, QG_W + 2 * KG_W + QW_W + KW_W)
WINDOW = 128
Q_BLOCK = 128
ROPE_THETA = 10000.0
NORM_EPS = 1e-6
N_EXPERTS = 64
N_EXPERT_GROUPS = 8
EXPERTS_PER_GROUP = N_EXPERTS // N_EXPERT_GROUPS
TOPK_GROUPS = 4
TOP_K = 8
D_EXPERT = 512
D_SHARED = 512
ROUTED_SCALE = 2.5
EXPERT_BLOCK = 128

kernel_name = "hybrid_dit_prefix_ctx_step"


def rms_norm(x, gain):
    xf = x.astype(jnp.float32)
    r = lax.rsqrt(jnp.mean(xf * xf, axis=-1, keepdims=True) + NORM_EPS)
    return (xf * r * gain.astype(jnp.float32)).astype(x.dtype)


def adaln_params(cond, w_ada, b_ada):
    mod = jax.nn.silu(cond) @ w_ada + b_ada
    return jnp.split(mod[:, None, :], 6, axis=-1)


def pre_modulate(x, gain, shift, scale):
    return rms_norm(x, gain) * (1 + scale) + shift


def axial_rope_tables(n_tokens):
    rows = n_tokens // GRID_W
    row = jnp.repeat(jnp.arange(rows, dtype=jnp.float32), GRID_W)
    col = jnp.tile(jnp.arange(GRID_W, dtype=jnp.float32), rows)
    inv_freq = ROPE_THETA ** (-jnp.arange(0, AXIS_DIM, 2, dtype=jnp.float32) / AXIS_DIM)
    ang_r = row[:, None] * inv_freq
    ang_c = col[:, None] * inv_freq
    ang = jnp.concatenate([ang_r, ang_r, ang_c, ang_c], axis=-1)
    return jnp.cos(ang), jnp.sin(ang)


def _rotate_half(h):
    a, b = jnp.split(h, 2, axis=-1)
    return jnp.concatenate([-b, a], axis=-1)


def apply_axial_rope(x, cos, sin):
    shape = (1, x.shape[1]) + (1,) * (x.ndim - 3) + (HEAD_DIM,)
    xf = x.astype(jnp.float32)
    xr, xc = jnp.split(xf, 2, axis=-1)
    rot = jnp.concatenate([_rotate_half(xr), _rotate_half(xc)], axis=-1)
    return (xf * cos.reshape(shape) + rot * sin.reshape(shape)).astype(x.dtype)


def project_heads(h, w_in, q_gain, k_gain):
    B, S, _ = h.shape
    proj = h @ w_in
    qg, kg, vg, qw, kw, vw = jnp.split(proj, QKV_SPLITS, axis=-1)
    qg = rms_norm(qg.reshape(B, S, N_KV_GLOB, G_GLOB, HEAD_DIM), q_gain)
    kg = rms_norm(kg.reshape(B, S, N_KV_GLOB, HEAD_DIM), k_gain)
    vg = vg.reshape(B, S, N_KV_GLOB, HEAD_DIM)
    qw = qw.reshape(B, S, N_KV_WIN, G_WIN, HEAD_DIM)
    kw = kw.reshape(B, S, N_KV_WIN, HEAD_DIM)
    vw = vw.reshape(B, S, N_KV_WIN, HEAD_DIM)
    return qg, kg, vg, qw, kw, vw


def attend(q, k, v, mask, sink):
    B, Q, KV, G, HD = q.shape
    s = jnp.einsum('bqngd,bknd->bngqk', q.astype(jnp.float32), k.astype(jnp.float32)) * (HEAD_DIM ** -0.5)
    if mask is not None:
        s = jnp.where(mask, s, -jnp.inf)
    if sink is not None:
        sk = jnp.broadcast_to(sink.astype(jnp.float32).reshape(1, KV, G, 1, 1), s.shape[:-1] + (1,))
        p = jax.nn.softmax(jnp.concatenate([sk, s], axis=-1), axis=-1)[..., 1:]
    else:
        p = jax.nn.softmax(s, axis=-1)
    o = jnp.einsum('bngqk,bknd->bqngd', p, v.astype(jnp.float32))
    return o.astype(q.dtype)


def dense_attention(q, k, v, sink):
    B, S, KV, G, HD = q.shape
    nb = S // Q_BLOCK
    qb = jnp.moveaxis(q.reshape(B, nb, Q_BLOCK, KV, G, HD), 1, 0)
    out = lax.map(lambda qi: attend(qi, k, v, None, sink), qb)
    return jnp.moveaxis(out, 0, 1).reshape(B, S, KV, G, HD)


def window_attention(q, k, v, k_ctx, v_ctx, sink):
    B, S, KV, G, HD = q.shape
    nb = S // Q_BLOCK
    pad = ((0, 0), (Q_BLOCK, Q_BLOCK), (0, 0), (0, 0))
    kp = jnp.pad(k, pad)
    vp = jnp.pad(v, pad)
    r = jnp.arange(Q_BLOCK)[:, None]
    j = jnp.arange(3 * Q_BLOCK)[None, :]
    rel_ok = jnp.abs(j - Q_BLOCK - r) <= WINDOW
    ctx_ok = jnp.ones((Q_BLOCK, k_ctx.shape[1]), dtype=bool)

    def block(i):
        qi = lax.dynamic_slice_in_dim(q, i * Q_BLOCK, Q_BLOCK, axis=1)
        ki = lax.dynamic_slice_in_dim(kp, i * Q_BLOCK, 3 * Q_BLOCK, axis=1)
        vi = lax.dynamic_slice_in_dim(vp, i * Q_BLOCK, 3 * Q_BLOCK, axis=1)
        kpos = i * Q_BLOCK - Q_BLOCK + j
        band_ok = rel_ok & (kpos >= 0) & (kpos < S)
        mask = jnp.concatenate([ctx_ok, band_ok], axis=1)
        return attend(qi, jnp.concatenate([k_ctx, ki], axis=1), jnp.concatenate([v_ctx, vi], axis=1), mask, sink)

    out = lax.map(block, jnp.arange(nb))
    return jnp.moveaxis(out, 0, 1).reshape(B, S, KV, G, HD)


def merge_heads(o_glob, o_win, w_out):
    B, S = o_glob.shape[:2]
    o = jnp.concatenate([o_glob.reshape(B, S, -1), o_win.reshape(B, S, -1)], axis=-1)
    return o @ w_out


def swiglu(x, w_gate, w_up, w_down):
    return (jax.nn.silu(x @ w_gate) * (x @ w_up)) @ w_down


def routed_experts(xf, topi, gates, w_gate_e, w_up_e, w_down_e):
    T, D = xf.shape
    A = T * TOP_K
    n_blocks = -(-(A + N_EXPERTS * (EXPERT_BLOCK - 1)) // EXPERT_BLOCK)
    n_rows = n_blocks * EXPERT_BLOCK
    e_flat = topi.reshape(A)
    tok_flat = jnp.arange(A, dtype=jnp.int32) // TOP_K
    order = jnp.argsort(e_flat)
    e_sorted = e_flat[order]
    counts = jnp.bincount(e_flat, length=N_EXPERTS)
    padded = (counts + EXPERT_BLOCK - 1) // EXPERT_BLOCK * EXPERT_BLOCK
    pad_end = jnp.cumsum(padded)
    pad_start = pad_end - padded
    start = jnp.cumsum(counts) - counts
    dest = pad_start[e_sorted] + jnp.arange(A, dtype=jnp.int32) - start[e_sorted]
    row_tok = jnp.full((n_rows,), T, dtype=jnp.int32).at[dest].set(tok_flat[order])
    row_gate = jnp.zeros((n_rows,), jnp.float32).at[dest].set(gates.reshape(A)[order])
    block_expert = jnp.minimum(
        jnp.searchsorted(pad_end, jnp.arange(n_blocks, dtype=jnp.int32) * EXPERT_BLOCK, side='right'),
        N_EXPERTS - 1)
    x_pad = jnp.concatenate([xf, jnp.zeros((1, D), xf.dtype)], axis=0)

    def block(args):
        toks, e = args
        return swiglu(x_pad[toks], w_gate_e[e], w_up_e[e], w_down_e[e])

    yb = lax.map(block, (row_tok.reshape(n_blocks, EXPERT_BLOCK), block_expert))
    y = jax.ops.segment_sum(yb.reshape(n_rows, D).astype(jnp.float32) * row_gate[:, None], row_tok,
                            num_segments=T + 1)[:T]
    return y.astype(xf.dtype)


def moe_ffn(h, w_router, router_bias, w_gate_e, w_up_e, w_down_e, w_gate_s, w_up_s, w_down_s):
    B, S, D = h.shape
    T = B * S
    xf = h.reshape(T, D)
    scores = jax.nn.sigmoid((xf @ w_router).astype(jnp.float32))
    sel = scores + router_bias.astype(jnp.float32)
    grp_score = lax.top_k(sel.reshape(T, N_EXPERT_GROUPS, EXPERTS_PER_GROUP), 2)[0].sum(-1)
    _, top_g = lax.top_k(grp_score, TOPK_GROUPS)
    gmask = jnp.any(top_g[..., None] == jnp.arange(N_EXPERT_GROUPS), axis=1)
    emask = jnp.repeat(gmask, EXPERTS_PER_GROUP, axis=1)
    _, topi = lax.top_k(jnp.where(emask, sel, -jnp.inf), TOP_K)
    w = jnp.take_along_axis(scores, topi, axis=1)
    w = w / jnp.sum(w, axis=-1, keepdims=True) * ROUTED_SCALE
    y = routed_experts(xf, topi, w, w_gate_e, w_up_e, w_down_e) + swiglu(xf, w_gate_s, w_up_s, w_down_s)
    return y.reshape(B, S, D)


def setup_inputs(seed: int = 0) -> dict:
    key = jax.random.key(seed)
    ks = jax.random.split(key, 32)
    f32 = jnp.float32
    D = D_MODEL
    nrm = lambda k, shape, s: jax.random.normal(k, shape, f32) * s
    gain = lambda k, shape: 1.0 + 0.1 * jax.random.normal(k, shape, f32)
    return {
        "x_prompt": nrm(ks[0], (BATCH, SEQ, D), 1.0),
        "x_sample": nrm(ks[1], (DEC_BATCH, DEC_SEQ, D), 1.0),
        "cache_glob_k": nrm(ks[2], (DEC_BATCH, DEPTH, PAST_LEN, N_KV_GLOB, HEAD_DIM), 1.0),
        "cache_glob_v": nrm(ks[3], (DEC_BATCH, DEPTH, PAST_LEN, N_KV_GLOB, HEAD_DIM), 1.0),
        "cache_win_k": nrm(ks[4], (DEC_BATCH, DEPTH, PAST_LEN, N_KV_WIN, HEAD_DIM), 1.0),
        "cache_win_v": nrm(ks[5], (DEC_BATCH, DEPTH, PAST_LEN, N_KV_WIN, HEAD_DIM), 1.0),
        "c": nrm(ks[6], (DEC_BATCH, D), 1.0),
        "c_ctx": nrm(ks[7], (D,), 1.0),
        "w_ada": nrm(ks[8], (DEPTH, D, 6 * D), 0.5 * D ** -0.5),
        "b_ada": nrm(ks[9], (DEPTH, 6 * D), 0.02),
        "attn_pre_g": gain(ks[10], (DEPTH, D)),
        "attn_post_g": gain(ks[11], (DEPTH, D)),
        "w_in": nrm(ks[12], (DEPTH, D, QKV_WIDTH), D ** -0.5),
        "q_norm_g": gain(ks[13], (DEPTH, HEAD_DIM)),
        "k_norm_g": gain(ks[14], (DEPTH, HEAD_DIM)),
        "sink_logit": nrm(ks[15], (DEPTH, N_HEADS_WIN), 0.5),
        "w_out": nrm(ks[16], (DEPTH, MIX_WIDTH, D), MIX_WIDTH ** -0.5),
        "ffn_pre_g": gain(ks[17], (DEPTH, D)),
        "ffn_post_g": gain(ks[18], (DEPTH, D)),
        "w_router": nrm(ks[19], (DEPTH, D, N_EXPERTS), D ** -0.5),
        "router_bias": nrm(ks[20], (DEPTH, N_EXPERTS), 0.01),
        "w_gate_e": nrm(ks[21], (DEPTH, N_EXPERTS, D, D_EXPERT), D ** -0.5),
        "w_up_e": nrm(ks[22], (DEPTH, N_EXPERTS, D, D_EXPERT), D ** -0.5),
        "w_down_e": nrm(ks[23], (DEPTH, N_EXPERTS, D_EXPERT, D), D_EXPERT ** -0.5),
        "w_gate_s": nrm(ks[24], (DEPTH, D, D_SHARED), D ** -0.5),
        "w_up_s": nrm(ks[25], (DEPTH, D, D_SHARED), D ** -0.5),
        "w_down_s": nrm(ks[26], (DEPTH, D_SHARED, D), D_SHARED ** -0.5),
    }


def reference(x_prompt, x_sample, cache_glob_k, cache_glob_v, cache_win_k, cache_win_v, c, c_ctx,
              w_ada, b_ada, attn_pre_g, attn_post_g, w_in, q_norm_g, k_norm_g, sink_logit, w_out,
              ffn_pre_g, ffn_post_g, w_router, router_bias, w_gate_e, w_up_e, w_down_e,
              w_gate_s, w_up_s, w_down_s):
    cos, sin = axial_rope_tables(x_sample.shape[1])
    y_p = x_prompt
    y_s = x_sample
    glob_k, glob_v, win_k, win_v = [], [], [], []
    for l in range(DEPTH):
        moe_w = (w_router[l], router_bias[l], w_gate_e[l], w_up_e[l], w_down_e[l],
                 w_gate_s[l], w_up_s[l], w_down_s[l])
        sh_a, sc_a, g_a, sh_f, sc_f, g_f = adaln_params(c_ctx[None, :], w_ada[l], b_ada[l])
        h = pre_modulate(y_p, attn_pre_g[l], sh_a, sc_a)
        qg, kg, vg, qw, kw, vw = project_heads(h, w_in[l], q_norm_g[l], k_norm_g[l])
        o = merge_heads(dense_attention(qg, kg, vg, None),
                        dense_attention(qw, kw, vw, sink_logit[l]), w_out[l])
        y_p = y_p + g_a * rms_norm(o, attn_post_g[l])
        glob_k.append(kg)
        glob_v.append(vg)
        win_k.append(kw)
        win_v.append(vw)
        h = pre_modulate(y_p, ffn_pre_g[l], sh_f, sc_f)
        y_p = y_p + g_f * rms_norm(moe_ffn(h, *moe_w), ffn_post_g[l])
        sh_a, sc_a, g_a, sh_f, sc_f, g_f = adaln_params(c, w_ada[l], b_ada[l])
        h = pre_modulate(y_s, attn_pre_g[l], sh_a, sc_a)
        qg, kg, vg, qw, kw, vw = project_heads(h, w_in[l], q_norm_g[l], k_norm_g[l])
        qg = apply_axial_rope(qg, cos, sin)
        kg = apply_axial_rope(kg, cos, sin)
        qw = apply_axial_rope(qw, cos, sin)
        kw = apply_axial_rope(kw, cos, sin)
        o_g = dense_attention(qg, jnp.concatenate([cache_glob_k[:, l], kg], axis=1),
                              jnp.concatenate([cache_glob_v[:, l], vg], axis=1), None)
        o_w = window_attention(qw, kw, vw, cache_win_k[:, l], cache_win_v[:, l], sink_logit[l])
        y_s = y_s + g_a * rms_norm(merge_heads(o_g, o_w, w_out[l]), attn_post_g[l])
        h = pre_modulate(y_s, ffn_pre_g[l], sh_f, sc_f)
        y_s = y_s + g_f * rms_norm(moe_ffn(h, *moe_w), ffn_post_g[l])
    new_glob_k = jnp.stack(glob_k, axis=1)
    new_glob_v = jnp.stack(glob_v, axis=1)
    new_win_k = jnp.stack(win_k, axis=1)
    new_win_v = jnp.stack(win_v, axis=1)
    return (y_p, y_s, new_glob_k, new_glob_v, new_win_k, new_win_v)
```

```python
import functools

import jax
import jax.numpy as jnp
from jax import lax
from jax.experimental import pallas as pl
from jax.experimental.pallas import tpu as pltpu

F32 = jnp.float32
BF16 = jnp.bfloat16
U32 = jnp.uint32
I32 = jnp.int32

D_MODEL = 2048
GRID_W = 64
HEAD_DIM = 128
AXIS_DIM = HEAD_DIM // 2
N_HEADS = 8
N_KV = 2
GROUP = N_HEADS // N_KV
Q_W = N_HEADS * HEAD_DIM
KV_W = N_KV * HEAD_DIM
QKV_WIDTH = 2 * Q_W + 4 * KV_W
WINDOW = 128
ROPE_THETA = 10000.0
NORM_EPS = 1e-6
N_EXPERTS = 64
N_GROUPS = 8
TOPK_GROUPS = 4
TOP_K = 8
D_EXPERT = 512
ROUTED_SCALE = 2.5

HALF = D_MODEL // 2
MOD_ROWS = 8
VMEM_LIMIT = 56 * 1024 * 1024

TM = 256
TQ = 256
TE = 256
NEG = -1e30
ISSUE_UNROLL = 32


def _cparams(sem):
    return pltpu.CompilerParams(dimension_semantics=sem, vmem_limit_bytes=VMEM_LIMIT)


def _pack_bf16_pair(lo, hi):
    lo_b = lax.bitcast_convert_type(lo.astype(BF16).astype(F32), U32)
    hi_b = lax.bitcast_convert_type(hi.astype(BF16).astype(F32), U32)
    return (hi_b & jnp.uint32(0xFFFF0000)) | (lo_b >> 16)


def _unpack_bf16_pair(p):
    lo = lax.bitcast_convert_type(p << 16, F32)
    hi = lax.bitcast_convert_type(p & jnp.uint32(0xFFFF0000), F32)
    return lo, hi


def _rms(x, gain):
    r = lax.rsqrt(jnp.mean(x * x, axis=-1, keepdims=True) + NORM_EPS)
    return x * r * gain


def _dot_nt(a, b):
    return lax.dot_general(a, b, (((1,), (1,)), ((), ())), preferred_element_type=F32)


def _adaln_kernel(cond_ref, w_ref, b_ref, o_ref):
    c = cond_ref[...]
    s = (c / (1.0 + jnp.exp(-c))).astype(BF16)
    o_ref[...] = jnp.dot(s, w_ref[...].astype(BF16), preferred_element_type=F32) + b_ref[...]


def _adaln(cond, w_ada, b_ada):
    n = w_ada.shape[1]
    tn = 1024
    return pl.pallas_call(
        _adaln_kernel,
        grid=(n // tn,),
        in_specs=[pl.BlockSpec((MOD_ROWS, D_MODEL), lambda j: (0, 0)),
                  pl.BlockSpec((D_MODEL, tn), lambda j: (0, j)),
                  pl.BlockSpec((1, tn), lambda j: (0, j))],
        out_specs=pl.BlockSpec((MOD_ROWS, tn), lambda j: (0, j)),
        out_shape=jax.ShapeDtypeStruct((MOD_ROWS, n), F32),
        compiler_params=_cparams(("arbitrary",)),
        name="adaln",
    )(cond, w_ada, b_ada)


def _mod_chunk(mod_ref, row, k):
    return mod_ref[pl.ds(row, 1), k * D_MODEL:(k + 1) * D_MODEL]


def _qkv_kernel(x_ref, mod_ref, g_ref, w_ref, qn_ref, kn_ref, cos_ref, sa_ref, sb_ref,
                qg_ref, kg_ref, vg_ref, qw_ref, kw_ref, vw_ref, *, rope, tiles_per_row):
    i = pl.program_id(0)
    row = (1 + i // tiles_per_row) if rope else 0
    x = x_ref[...]
    h = _rms(x, g_ref[...]) * (1.0 + _mod_chunk(mod_ref, row, 1)) + _mod_chunk(mod_ref, row, 0)
    proj = jnp.dot(h.astype(BF16), w_ref[...], preferred_element_type=F32)

    if rope:
        cos, sa, sb = cos_ref[...], sa_ref[...], sb_ref[...]

    def rot(t):
        if not rope:
            return t
        return t * cos + pltpu.roll(t, 96, 1) * sa + pltpu.roll(t, 32, 1) * sb

    scale = HEAD_DIM ** -0.5
    qn, kn = qn_ref[...], kn_ref[...]
    off = 0
    for hd in range(N_HEADS):
        t = proj[:, off + hd * HEAD_DIM: off + (hd + 1) * HEAD_DIM]
        qg_ref[:, hd * HEAD_DIM:(hd + 1) * HEAD_DIM] = (rot(_rms(t, qn)) * scale).astype(qg_ref.dtype)
    off += Q_W
    for hd in range(N_KV):
        t = proj[:, off + hd * HEAD_DIM: off + (hd + 1) * HEAD_DIM]
        kg_ref[:, hd * HEAD_DIM:(hd + 1) * HEAD_DIM] = rot(_rms(t, kn)).astype(kg_ref.dtype)
    off += KV_W
    vg_ref[...] = proj[:, off:off + KV_W].astype(vg_ref.dtype)
    off += KV_W
    for hd in range(N_HEADS):
        t = proj[:, off + hd * HEAD_DIM: off + (hd + 1) * HEAD_DIM]
        qw_ref[:, hd * HEAD_DIM:(hd + 1) * HEAD_DIM] = (rot(t) * scale).astype(qw_ref.dtype)
    off += Q_W
    for hd in range(N_KV):
        t = proj[:, off + hd * HEAD_DIM: off + (hd + 1) * HEAD_DIM]
        kw_ref[:, hd * HEAD_DIM:(hd + 1) * HEAD_DIM] = rot(t).astype(kw_ref.dtype)
    off += KV_W
    vw_ref[...] = proj[:, off:off + KV_W].astype(vw_ref.dtype)


def _qkv(x2d, mod, pre_g, w_in_b, qn, kn, cos, sa, sb, *, rope, seq, kv_dtype):
    t = x2d.shape[0]
    tiles_per_row = seq // TM
    full = lambda shape: pl.BlockSpec(shape, lambda i: (0,) * len(shape))
    tab = pl.BlockSpec((TM, HEAD_DIM), (lambda i: (i % tiles_per_row, 0)) if rope else (lambda i: (0, 0)))
    tok = lambda w: pl.BlockSpec((TM, w), lambda i: (i, 0))
    return pl.pallas_call(
        functools.partial(_qkv_kernel, rope=rope, tiles_per_row=tiles_per_row),
        grid=(t // TM,),
        in_specs=[tok(D_MODEL), full(mod.shape), full((1, D_MODEL)),
                  full((D_MODEL, QKV_WIDTH)), full((1, HEAD_DIM)), full((1, HEAD_DIM)),
                  tab, tab, tab],
        out_specs=[tok(Q_W), tok(KV_W), tok(KV_W), tok(Q_W), tok(KV_W), tok(KV_W)],
        out_shape=[jax.ShapeDtypeStruct((t, Q_W), BF16),
                   jax.ShapeDtypeStruct((t, KV_W), kv_dtype),
                   jax.ShapeDtypeStruct((t, KV_W), kv_dtype),
                   jax.ShapeDtypeStruct((t, Q_W), BF16),
                   jax.ShapeDtypeStruct((t, KV_W), kv_dtype),
                   jax.ShapeDtypeStruct((t, KV_W), kv_dtype)],
        compiler_params=_cparams(("arbitrary",)),
        name="qkv_rope" if rope else "qkv_ctx",
    )(x2d, mod, pre_g, w_in_b, qn, kn, cos, sa, sb)


def _softmax_pv(scores, values, sink):
    m = scores[0].max(axis=-1, keepdims=True)
    for s in scores[1:]:
        m = jnp.maximum(m, s.max(axis=-1, keepdims=True))
    if sink is not None:
        m = jnp.maximum(m, sink)
    den = jnp.exp(sink - m) if sink is not None else 0.0
    acc = None
    for s, v in zip(scores, values):
        p = jnp.exp(s - m)
        den = den + p.sum(axis=-1, keepdims=True)
        pv = jnp.dot(p.astype(BF16), v, preferred_element_type=F32)
        acc = pv if acc is None else acc + pv
    return acc / den


def _ctx_attn_kernel(sink_ref, qg_ref, kg_ref, vg_ref, qw_ref, kw_ref, vw_ref, o_ref):
    for mixer, (q_ref, k_ref, v_ref) in enumerate(((qg_ref, kg_ref, vg_ref), (qw_ref, kw_ref, vw_ref))):
        for n in range(N_KV):
            k = k_ref[:, n * HEAD_DIM:(n + 1) * HEAD_DIM].astype(BF16)
            v = v_ref[:, n * HEAD_DIM:(n + 1) * HEAD_DIM].astype(BF16)
            for g in range(GROUP):
                hd = n * GROUP + g
                q = q_ref[:, hd * HEAD_DIM:(hd + 1) * HEAD_DIM]
                sink = sink_ref[hd] if mixer == 1 else None
                o = _softmax_pv([_dot_nt(q, k)], [v], sink)
                c0 = mixer * Q_W + hd * HEAD_DIM
                o_ref[:, c0:c0 + HEAD_DIM] = o.astype(o_ref.dtype)


def _ctx_attention(sink, qg, kg, vg, qw, kw, vw, *, seq):
    t = qg.shape[0]
    blk = lambda w: pl.BlockSpec((seq, w), lambda b: (b, 0))
    return pl.pallas_call(
        _ctx_attn_kernel,
        grid=(t // seq,),
        in_specs=[pl.BlockSpec(memory_space=pltpu.SMEM),
                  blk(Q_W), blk(KV_W), blk(KV_W), blk(Q_W), blk(KV_W), blk(KV_W)],
        out_specs=blk(2 * Q_W),
        out_shape=jax.ShapeDtypeStruct((t, 2 * Q_W), BF16),
        compiler_params=_cparams(("arbitrary",)),
        name="attn_ctx",
    )(sink, qg, kg, vg, qw, kw, vw)


def _lat_attn_kernel(sink_ref, qg_ref, kg_ref, vg_ref, qw_ref, kw_ref, vw_ref,
                     cgk_ref, cgv_ref, cwk_ref, cwv_ref, o_ref, *, seq):
    qi = pl.program_id(1)
    span = TQ + 2 * WINDOW
    start = pl.multiple_of(jnp.clip(qi * TQ - WINDOW, 0, seq - span), WINDOW)
    qpos = qi * TQ + lax.broadcasted_iota(I32, (TQ, span), 0)
    kpos = start + lax.broadcasted_iota(I32, (TQ, span), 1)
    band_ok = jnp.abs(kpos - qpos) <= WINDOW
    for n in range(N_KV):
        cols = slice(n * HEAD_DIM, (n + 1) * HEAD_DIM)
        kc, vc = cgk_ref[:, cols].astype(BF16), cgv_ref[:, cols].astype(BF16)
        kl, vl = kg_ref[:, cols], vg_ref[:, cols]
        for g in range(GROUP):
            hd = n * GROUP + g
            q = qg_ref[:, hd * HEAD_DIM:(hd + 1) * HEAD_DIM]
            o = _softmax_pv([_dot_nt(q, kc), _dot_nt(q, kl)], [vc, vl], None)
            o_ref[:, hd * HEAD_DIM:(hd + 1) * HEAD_DIM] = o.astype(o_ref.dtype)
        kc, vc = cwk_ref[:, cols].astype(BF16), cwv_ref[:, cols].astype(BF16)
        kl, vl = kw_ref[pl.ds(start, span), cols], vw_ref[pl.ds(start, span), cols]
        for g in range(GROUP):
            hd = n * GROUP + g
            q = qw_ref[:, hd * HEAD_DIM:(hd + 1) * HEAD_DIM]
            band = jnp.where(band_ok, _dot_nt(q, kl), NEG)
            o = _softmax_pv([_dot_nt(q, kc), band], [vc, vl], sink_ref[hd])
            c0 = Q_W + hd * HEAD_DIM
            o_ref[:, c0:c0 + HEAD_DIM] = o.astype(o_ref.dtype)


def _lat_attention(sink, qg, kg, vg, qw, kw, vw, cgk, cgv, cwk, cwv, *, seq, past):
    t = qg.shape[0]
    nq = seq // TQ
    qblk = pl.BlockSpec((TQ, Q_W), lambda b, i: (b * nq + i, 0))
    kblk = pl.BlockSpec((seq, KV_W), lambda b, i: (b, 0))
    cblk = pl.BlockSpec((past, KV_W), lambda b, i: (b, 0))
    return pl.pallas_call(
        functools.partial(_lat_attn_kernel, seq=seq),
        grid=(t // seq, nq),
        in_specs=[pl.BlockSpec(memory_space=pltpu.SMEM),
                  qblk, kblk, kblk, qblk, kblk, kblk, cblk, cblk, cblk, cblk],
        out_specs=pl.BlockSpec((TQ, 2 * Q_W), lambda b, i: (b * nq + i, 0)),
        out_shape=jax.ShapeDtypeStruct((t, 2 * Q_W), BF16),
        compiler_params=_cparams(("arbitrary", "arbitrary")),
        name="attn_lat",
    )(sink, qg, kg, vg, qw, kw, vw, cgk, cgv, cwk, cwv)


def _route(sel, scores, tm):
    per = N_EXPERTS // N_GROUPS
    sel3 = sel.reshape(N_GROUPS, per, tm)
    sc3 = scores.reshape(N_GROUPS, per, tm)
    member = lax.broadcasted_iota(I32, (N_GROUPS, per, tm), 1)
    m1 = sel3.max(axis=1, keepdims=True)
    first = jnp.where(sel3 == m1, member, per).min(axis=1, keepdims=True)
    m2 = jnp.where(member == first, -jnp.inf, sel3).max(axis=1, keepdims=True)
    gs = m1 + m2

    gid = lax.broadcasted_iota(I32, (N_GROUPS, 1, tm), 0)
    chosen = jnp.zeros((N_GROUPS, 1, tm), F32)
    for _ in range(TOPK_GROUPS):
        gm = gs.max(axis=0, keepdims=True)
        gfirst = jnp.where(gs == gm, gid, N_GROUPS).min(axis=0, keepdims=True)
        hit = gid == gfirst
        chosen = jnp.where(hit, 1.0, chosen)
        gs = jnp.where(hit, -jnp.inf, gs)

    eid = lax.broadcasted_iota(I32, (N_GROUPS, per, tm), 0) * per + member
    masked = jnp.where(chosen > 0.0, sel3, -jnp.inf)
    ids, raw = [], []
    for _ in range(TOP_K):
        mx = masked.max(axis=1, keepdims=True).max(axis=0, keepdims=True)
        efirst = jnp.where(masked == mx, eid, N_EXPERTS).min(axis=1, keepdims=True).min(axis=0, keepdims=True)
        hit = eid == efirst
        ids.append(efirst.reshape(1, tm))
        raw.append(jnp.where(hit, sc3, 0.0).sum(axis=1, keepdims=True).sum(axis=0, keepdims=True).reshape(1, tm))
        masked = jnp.where(hit, -jnp.inf, masked)
    ids = jnp.concatenate(ids, axis=0)
    raw = jnp.concatenate(raw, axis=0)
    gates = raw / raw.sum(axis=0, keepdims=True) * ROUTED_SCALE
    return ids, gates


def _post_attn_kernel(op_ref, os_ref, xp_ref, xs_ref, mod_ref, w_ref, pg_ref, fg_ref,
                      wrh_ref, wrl_ref, rb_ref,
                      y_ref, h_ref, ids_ref, gates_ref, *, ctx_tiles, tiles_per_row):
    i = pl.program_id(0)
    is_ctx = i < ctx_tiles
    row = jnp.where(is_ctx, 0, 1 + (i - ctx_tiles) // tiles_per_row)
    o = jnp.where(is_ctx, op_ref[...], os_ref[...])
    x = jnp.where(is_ctx, xp_ref[...], xs_ref[...])
    a = jnp.dot(o, w_ref[...], preferred_element_type=F32)
    y = x + _mod_chunk(mod_ref, row, 2) * _rms(a, pg_ref[...])
    y_ref[...] = y
    h = _rms(y, fg_ref[...]) * (1.0 + _mod_chunk(mod_ref, row, 4)) + _mod_chunk(mod_ref, row, 3)
    h_ref[...] = _pack_bf16_pair(h[:, :HALF], h[:, HALF:])
    h_hi = h.astype(BF16)
    h_lo = (h - h_hi.astype(F32)).astype(BF16)
    wh, wl = wrh_ref[...], wrl_ref[...]
    logits = _dot_nt(wh, h_hi) + (_dot_nt(wl, h_hi) + _dot_nt(wh, h_lo))
    scores = 1.0 / (1.0 + jnp.exp(-logits))
    ids, gates = _route(scores + rb_ref[...], scores, logits.shape[1])
    ids_ref[...] = ids
    gates_ref[...] = gates


def _post_attn(o_p, o_s, x_p, x_s, mod, w_out_b, post_g, ffn_g, wr_hi, wr_lo, rbias, *, seq):
    tp, ts = x_p.shape[0], x_s.shape[0]
    t = tp + ts
    ctx_tiles = tp // TM
    full = lambda shape: pl.BlockSpec(shape, lambda i: (0,) * len(shape))
    ctx = lambda w: pl.BlockSpec((TM, w), lambda i: (jnp.minimum(i, ctx_tiles - 1), 0))
    lat = lambda w: pl.BlockSpec((TM, w), lambda i: (jnp.maximum(i - ctx_tiles, 0), 0))
    tok = lambda w: pl.BlockSpec((TM, w), lambda i: (i, 0))
    lane = pl.BlockSpec((TOP_K, TM), lambda i: (0, i))
    return pl.pallas_call(
        functools.partial(_post_attn_kernel, ctx_tiles=ctx_tiles, tiles_per_row=seq // TM),
        grid=(t // TM,),
        in_specs=[ctx(D_MODEL), lat(D_MODEL), ctx(D_MODEL), lat(D_MODEL), full(mod.shape),
                  full((D_MODEL, D_MODEL)), full((1, D_MODEL)), full((1, D_MODEL)),
                  full((N_EXPERTS, D_MODEL)), full((N_EXPERTS, D_MODEL)), full((N_EXPERTS, 1))],
        out_specs=[tok(D_MODEL), tok(HALF), lane, lane],
        out_shape=[jax.ShapeDtypeStruct((t, D_MODEL), F32),
                   jax.ShapeDtypeStruct((t, HALF), U32),
                   jax.ShapeDtypeStruct((TOP_K, t), I32),
                   jax.ShapeDtypeStruct((TOP_K, t), F32)],
        compiler_params=_cparams(("arbitrary",)),
        name="post_attn_router",
    )(o_p, o_s, x_p, x_s, mod, w_out_b, post_g, ffn_g, wr_hi, wr_lo, rbias)


def _issue_row_gather(idx_ref, n, src_hbm, dst_of_row, sem):
    def chunk(c, carry):
        for u in range(ISSUE_UNROLL):
            r = c * ISSUE_UNROLL + u
            pltpu.make_async_copy(src_hbm.at[pl.ds(idx_ref[r], 1)], dst_of_row(r), sem).start()
        return carry
    lax.fori_loop(0, n // ISSUE_UNROLL, chunk, 0)


def _experts_kernel(be_ref, nused_ref, cur_ref, nxt_ref, h_hbm, wg_ref, wu_ref, wd_ref, out_ref,
                    idx_smem, xbuf, isem, gsem, wgb, wub, wdb):
    b = pl.program_id(0)
    n_used = nused_ref[0]
    slot = b % 2
    nslot = 1 - slot

    def gather(idx_vmem, s):
        cp = pltpu.make_async_copy(idx_vmem.at[0, 0], idx_smem, isem)
        cp.start()
        cp.wait()
        _issue_row_gather(idx_smem, TE, h_hbm, lambda r: xbuf.at[s, pl.ds(r, 1)], gsem.at[s])

    @pl.when(b == 0)
    def _():
        gather(cur_ref, 0)

    @pl.when(b + 1 < n_used)
    def _():
        gather(nxt_ref, nslot)

    @pl.when(b < n_used)
    def _():
        prev = be_ref[jnp.maximum(b - 1, 0)]

        @pl.when((b == 0) | (be_ref[b] != prev))
        def _():
            wgb[...] = wg_ref[0].astype(BF16)
            wub[...] = wu_ref[0].astype(BF16)
            wdb[...] = wd_ref[0].astype(BF16)

        pltpu.make_async_copy(h_hbm.at[pl.ds(0, TE)], xbuf.at[slot], gsem.at[slot]).wait()
        lo, hi = _unpack_bf16_pair(xbuf[slot])
        lo, hi = lo.astype(BF16), hi.astype(BF16)
        gate = (jnp.dot(lo, wgb[:HALF], preferred_element_type=F32)
                + jnp.dot(hi, wgb[HALF:], preferred_element_type=F32))
        up = (jnp.dot(lo, wub[:HALF], preferred_element_type=F32)
              + jnp.dot(hi, wub[HALF:], preferred_element_type=F32))
        act = (gate / (1.0 + jnp.exp(-gate)) * up).astype(BF16)
        y = jnp.dot(act, wdb[...], preferred_element_type=F32)
        out_ref[...] = _pack_bf16_pair(y[:, :HALF], y[:, HALF:])

    @pl.when(b >= n_used)
    def _():
        out_ref[...] = jnp.zeros_like(out_ref)


def _experts(block_expert, n_used, row_tok, h_packed, w_gate_e, w_up_e, w_down_e):
    n_blocks = row_tok.shape[0]
    wspec = lambda shape: pl.BlockSpec((1,) + shape, lambda b, be, nu: (be[b], 0, 0))
    grid_spec = pltpu.PrefetchScalarGridSpec(
        num_scalar_prefetch=2,
        grid=(n_blocks,),
        in_specs=[pl.BlockSpec((1, 1, TE), lambda b, be, nu: (b, 0, 0)),
                  pl.BlockSpec((1, 1, TE), lambda b, be, nu: (jnp.minimum(b + 1, n_blocks - 1), 0, 0)),
                  pl.BlockSpec(memory_space=pl.ANY),
                  wspec((D_MODEL, D_EXPERT)), wspec((D_MODEL, D_EXPERT)), wspec((D_EXPERT, D_MODEL))],
        out_specs=pl.BlockSpec((TE, HALF), lambda b, be, nu: (b, 0)),
        scratch_shapes=[pltpu.SMEM((TE,), I32),
                        pltpu.VMEM((2, TE, HALF), U32),
                        pltpu.SemaphoreType.DMA,
                        pltpu.SemaphoreType.DMA((2,)),
                        pltpu.VMEM((D_MODEL, D_EXPERT), BF16),
                        pltpu.VMEM((D_MODEL, D_EXPERT), BF16),
                        pltpu.VMEM((D_EXPERT, D_MODEL), BF16)])
    return pl.pallas_call(
        _experts_kernel,
        grid_spec=grid_spec,
        out_shape=jax.ShapeDtypeStruct((n_blocks * TE, HALF), U32),
        compiler_params=_cparams(("arbitrary",)),
        name="experts",
    )(block_expert, n_used, row_tok, row_tok, h_packed, w_gate_e, w_up_e, w_down_e)


def _ffn_out_kernel(cur_ref, nxt_ref, yb_hbm, gates_ref, h_ref, y1_ref, mod_ref, fg_ref,
                    wg_ref, wu_ref, wd_ref, out_ref, idx_smem, gbuf, isem, gsem,
                    *, n_tiles, tiles_per_row, mod_row0):
    i = pl.program_id(0)
    slot = i % 2
    nslot = 1 - slot
    n_idx = TOP_K * TM

    def gather(idx_vmem, s):
        cp = pltpu.make_async_copy(idx_vmem.at[0, 0], idx_smem, isem)
        cp.start()
        cp.wait()
        _issue_row_gather(idx_smem, n_idx, yb_hbm, lambda r: gbuf.at[s, pl.ds(r, 1)], gsem.at[s])

    @pl.when(i == 0)
    def _():
        gather(cur_ref, 0)

    @pl.when(i + 1 < n_tiles)
    def _():
        gather(nxt_ref, nslot)

    lo, hi = _unpack_bf16_pair(h_ref[...])
    lo, hi = lo.astype(BF16), hi.astype(BF16)
    gate = (jnp.dot(lo, wg_ref[:HALF], preferred_element_type=F32)
            + jnp.dot(hi, wg_ref[HALF:], preferred_element_type=F32))
    up = (jnp.dot(lo, wu_ref[:HALF], preferred_element_type=F32)
          + jnp.dot(hi, wu_ref[HALF:], preferred_element_type=F32))
    act = (gate / (1.0 + jnp.exp(-gate)) * up).astype(BF16)
    shared = jnp.dot(act, wd_ref[...], preferred_element_type=F32)

    pltpu.make_async_copy(yb_hbm.at[pl.ds(0, n_idx)], gbuf.at[slot], gsem.at[slot]).wait()
    g = gates_ref[...]
    acc_lo = shared[:, :HALF]
    acc_hi = shared[:, HALF:]
    for j in range(TOP_K):
        r_lo, r_hi = _unpack_bf16_pair(gbuf[slot, j * TM:(j + 1) * TM, :])
        acc_lo = acc_lo + g[:, j:j + 1] * r_lo
        acc_hi = acc_hi + g[:, j:j + 1] * r_hi
    ffn = jnp.concatenate([acc_lo, acc_hi], axis=-1)
    row = mod_row0 + (i // tiles_per_row if tiles_per_row else 0)
    out_ref[...] = y1_ref[...] + _mod_chunk(mod_ref, row, 5) * _rms(ffn, fg_ref[...])


def _ffn_out(dest_tiles, yb, gates_t, h_packed, y1, mod, post_g, wg_b, wu_b, wd_b,
             *, tile0, n_tiles, tiles_per_row, mod_row0):
    full = lambda shape: pl.BlockSpec(shape, lambda i: (0,) * len(shape))
    tok = lambda w: pl.BlockSpec((TM, w), lambda i: (tile0 + i, 0))
    last = tile0 + n_tiles - 1
    idx = lambda step: pl.BlockSpec((1, 1, TOP_K * TM),
                                    lambda i: (jnp.minimum(tile0 + i + step, last), 0, 0))
    return pl.pallas_call(
        functools.partial(_ffn_out_kernel, n_tiles=n_tiles,
                          tiles_per_row=tiles_per_row, mod_row0=mod_row0),
        grid=(n_tiles,),
        in_specs=[idx(0), idx(1), pl.BlockSpec(memory_space=pl.ANY),
                  tok(TOP_K), tok(HALF), tok(D_MODEL), full(mod.shape), full((1, D_MODEL)),
                  full((D_MODEL, D_EXPERT)), full((D_MODEL, D_EXPERT)), full((D_EXPERT, D_MODEL))],
        out_specs=pl.BlockSpec((TM, D_MODEL), lambda i: (i, 0)),
        out_shape=jax.ShapeDtypeStruct((n_tiles * TM, D_MODEL), F32),
        scratch_shapes=[pltpu.SMEM((TOP_K * TM,), I32),
                        pltpu.VMEM((2, TOP_K * TM, HALF), U32),
                        pltpu.SemaphoreType.DMA,
                        pltpu.SemaphoreType.DMA((2,))],
        compiler_params=_cparams(("arbitrary",)),
        name="ffn_out",
    )(dest_tiles, dest_tiles, yb, gates_t, h_packed, y1, mod, post_g, wg_b, wu_b, wd_b)


def _routing_tables(ids, n_blocks):
    t = ids.shape[1]
    ids_t = ids.T
    onehot = (ids_t[:, :, None] == jnp.arange(N_EXPERTS, dtype=I32)).sum(axis=1).astype(I32)
    before = jnp.cumsum(onehot, axis=0) - onehot
    counts = onehot.sum(axis=0)
    padded = (counts + TE - 1) // TE * TE
    pad_end = jnp.cumsum(padded)
    pad_start = pad_end - padded
    dest = pad_start[ids_t] + jnp.take_along_axis(before, ids_t, axis=1)
    tok = jnp.broadcast_to(jnp.arange(t, dtype=I32)[:, None], dest.shape)
    row_tok = jnp.zeros((n_blocks * TE,), I32).at[dest.reshape(-1)].set(tok.reshape(-1))
    block_row0 = jnp.arange(n_blocks, dtype=I32) * TE
    block_expert = jnp.minimum((pad_end[None, :] <= block_row0[:, None]).sum(axis=1),
                               N_EXPERTS - 1).astype(I32)
    n_used = (pad_end[-1:] // TE).astype(I32)
    return row_tok.reshape(n_blocks, 1, TE), block_expert, n_used, dest.astype(I32)


def _rope_tables(n_tokens):
    rows = n_tokens // GRID_W
    row = jnp.repeat(jnp.arange(rows, dtype=F32), GRID_W)
    col = jnp.tile(jnp.arange(GRID_W, dtype=F32), rows)
    inv_freq = ROPE_THETA ** (-jnp.arange(0, AXIS_DIM, 2, dtype=F32) / AXIS_DIM)
    ang_r = row[:, None] * inv_freq
    ang_c = col[:, None] * inv_freq
    ang = jnp.concatenate([ang_r, ang_r, ang_c, ang_c], axis=-1)
    cos, sin = jnp.cos(ang), jnp.sin(ang)
    first = (jnp.arange(HEAD_DIM) % AXIS_DIM) < (AXIS_DIM // 2)
    return cos, jnp.where(first, -sin, 0.0), jnp.where(first, 0.0, sin)


def kernel(x_prompt, x_sample, cache_glob_k, cache_glob_v, cache_win_k, cache_win_v, c, c_ctx,
           w_ada, b_ada, attn_pre_g, attn_post_g, w_in, q_norm_g, k_norm_g, sink_logit, w_out,
           ffn_pre_g, ffn_post_g, w_router, router_bias, w_gate_e, w_up_e, w_down_e,
           w_gate_s, w_up_s, w_down_s):
    batch, seq, d = x_prompt.shape
    dec_batch, dec_seq, _ = x_sample.shape
    past = cache_glob_k.shape[2]
    tp, ts = batch * seq, dec_batch * dec_seq
    t = tp + ts
    l = 0

    xp = x_prompt.reshape(tp, d)
    xs = x_sample.reshape(ts, d)
    cond = jnp.concatenate([c_ctx[None, :], c, jnp.zeros((MOD_ROWS - 1 - dec_batch, d), F32)], axis=0)
    mod = _adaln(cond, w_ada[l], b_ada[l][None, :])

    w_in_b = w_in[l].astype(BF16)
    pre_g = attn_pre_g[l][None, :]
    qn, kn = q_norm_g[l][None, :], k_norm_g[l][None, :]
    cos, sa, sb = _rope_tables(dec_seq)
    sink = sink_logit[l]

    qg, kg, vg, qw, kw, vw = _qkv(xp, mod, pre_g, w_in_b, qn, kn, cos, sa, sb,
                                  rope=False, seq=seq, kv_dtype=F32)
    o_p = _ctx_attention(sink, qg, kg, vg, qw, kw, vw, seq=seq)
    new_kv = [a.reshape(batch, 1, seq, N_KV, HEAD_DIM) for a in (kg, vg, kw, vw)]

    qg, kg, vg, qw, kw, vw = _qkv(xs, mod, pre_g, w_in_b, qn, kn, cos, sa, sb,
                                  rope=True, seq=dec_seq, kv_dtype=BF16)
    caches = [a[:, l].reshape(dec_batch * past, KV_W)
              for a in (cache_glob_k, cache_glob_v, cache_win_k, cache_win_v)]
    o_s = _lat_attention(sink, qg, kg, vg, qw, kw, vw, *caches, seq=dec_seq, past=past)

    wr_t = w_router[l].T
    wr_hi = wr_t.astype(BF16)
    wr_lo = (wr_t - wr_hi.astype(F32)).astype(BF16)
    y1, h_packed, ids, gates = _post_attn(
        o_p, o_s, xp, xs, mod, w_out[l].astype(BF16), attn_post_g[l][None, :], ffn_pre_g[l][None, :],
        wr_hi, wr_lo, router_bias[l][:, None], seq=dec_seq)

    n_blocks = -(-(t * TOP_K + N_EXPERTS * (TE - 1)) // TE)
    row_tok, block_expert, n_used, dest = _routing_tables(ids, n_blocks)
    yb = _experts(block_expert, n_used, row_tok, h_packed, w_gate_e[l], w_up_e[l], w_down_e[l])

    dest_tiles = dest.reshape(t // TM, TM, TOP_K).transpose(0, 2, 1).reshape(t // TM, 1, TOP_K * TM)
    gates_t = gates.T
    shared = (w_gate_s[l].astype(BF16), w_up_s[l].astype(BF16), w_down_s[l].astype(BF16))
    post_g = ffn_post_g[l][None, :]
    y_p = _ffn_out(dest_tiles, yb, gates_t, h_packed, y1, mod, post_g, *shared,
                   tile0=0, n_tiles=tp // TM, tiles_per_row=0, mod_row0=0)
    y_s = _ffn_out(dest_tiles, yb, gates_t, h_packed, y1, mod, post_g, *shared,
                   tile0=tp // TM, n_tiles=ts // TM, tiles_per_row=dec_seq // TM, mod_row0=1)
    return (y_p.reshape(batch, seq, d), y_s.reshape(dec_batch, dec_seq, d), *new_kv)
```

```python
import functools

import jax
import jax.numpy as jnp
from jax import lax
from jax.experimental import pallas as pl
from jax.experimental.pallas import tpu as pltpu

F32 = jnp.float32
BF16 = jnp.bfloat16
U32 = jnp.uint32
I32 = jnp.int32

D_MODEL = 2048
GRID_W = 64
HEAD_DIM = 128
AXIS_DIM = HEAD_DIM // 2
N_HEADS = 8
N_KV = 2
GROUP = N_HEADS // N_KV
Q_W = N_HEADS * HEAD_DIM
KV_W = N_KV * HEAD_DIM
QKV_WIDTH = 2 * Q_W + 4 * KV_W
WINDOW = 128
ROPE_THETA = 10000.0
NORM_EPS = 1e-6
N_EXPERTS = 64
N_GROUPS = 8
TOPK_GROUPS = 4
TOP_K = 8
D_EXPERT = 512
ROUTED_SCALE = 2.5

HALF = D_MODEL // 2
LANES = 128
CHUNKS = HALF // LANES
MOD_ROWS = 8
VMEM_LIMIT = 56 * 1024 * 1024

TM = 256
TQ = 256
TE = 256
NEG = -1e30
ISSUE_UNROLL = 32


def _cparams(sem):
    return pltpu.CompilerParams(dimension_semantics=sem, vmem_limit_bytes=VMEM_LIMIT)


def _pack_bf16_pair(lo, hi):
    lo_b = lax.bitcast_convert_type(lo.astype(BF16).astype(F32), U32)
    hi_b = lax.bitcast_convert_type(hi.astype(BF16).astype(F32), U32)
    return (hi_b & jnp.uint32(0xFFFF0000)) | (lo_b >> 16)


def _unpack_bf16_pair(p):
    lo = lax.bitcast_convert_type(p << 16, F32)
    hi = lax.bitcast_convert_type(p & jnp.uint32(0xFFFF0000), F32)
    return lo, hi


def _store_rows(ref, n, packed):
    for c in range(CHUNKS):
        ref[pl.ds(c, n, stride=CHUNKS), :] = packed[:, c * LANES:(c + 1) * LANES]


def _load_rows(ref, n, first=0):
    return jnp.concatenate(
        [ref[pl.ds(first * CHUNKS + c, n, stride=CHUNKS), :] for c in range(CHUNKS)], axis=-1)


def _rms(x, gain):
    r = lax.rsqrt(jnp.mean(x * x, axis=-1, keepdims=True) + NORM_EPS)
    return x * r * gain


def _dot_nt(a, b):
    return lax.dot_general(a, b, (((1,), (1,)), ((), ())), preferred_element_type=F32)


def _swiglu_packed(p, wg, wu, wd):
    lo, hi = _unpack_bf16_pair(p)
    lo, hi = lo.astype(BF16), hi.astype(BF16)
    gate = (jnp.dot(lo, wg[:HALF], preferred_element_type=F32)
            + jnp.dot(hi, wg[HALF:], preferred_element_type=F32))
    up = (jnp.dot(lo, wu[:HALF], preferred_element_type=F32)
          + jnp.dot(hi, wu[HALF:], preferred_element_type=F32))
    act = (gate / (1.0 + jnp.exp(-gate)) * up).astype(BF16)
    return jnp.dot(act, wd[...], preferred_element_type=F32)


def _adaln_kernel(cond_ref, w_ref, b_ref, o_ref):
    c = cond_ref[...]
    s = (c / (1.0 + jnp.exp(-c))).astype(BF16)
    o_ref[...] = jnp.dot(s, w_ref[...].astype(BF16), preferred_element_type=F32) + b_ref[...]


def _adaln(cond, w_ada, b_ada):
    n = w_ada.shape[1]
    tn = 1024
    return pl.pallas_call(
        _adaln_kernel,
        grid=(n // tn,),
        in_specs=[pl.BlockSpec((MOD_ROWS, D_MODEL), lambda j: (0, 0)),
                  pl.BlockSpec((D_MODEL, tn), lambda j: (0, j)),
                  pl.BlockSpec((1, tn), lambda j: (0, j))],
        out_specs=pl.BlockSpec((MOD_ROWS, tn), lambda j: (0, j)),
        out_shape=jax.ShapeDtypeStruct((MOD_ROWS, n), F32),
        compiler_params=_cparams(("arbitrary",)),
        name="adaln",
    )(cond, w_ada, b_ada)


def _mod_chunk(mod_ref, row, k):
    return mod_ref[pl.ds(row, 1), k * D_MODEL:(k + 1) * D_MODEL]


def _qkv_kernel(x_ref, mod_ref, g_ref, w_ref, qn_ref, kn_ref, cos_ref, sa_ref, sb_ref,
                qg_ref, kg_ref, vg_ref, qw_ref, kw_ref, vw_ref, *, rope, tiles_per_row):
    i = pl.program_id(0)
    row = (1 + i // tiles_per_row) if rope else 0
    x = x_ref[...]
    h = _rms(x, g_ref[...]) * (1.0 + _mod_chunk(mod_ref, row, 1)) + _mod_chunk(mod_ref, row, 0)
    proj = jnp.dot(h.astype(BF16), w_ref[...], preferred_element_type=F32)

    if rope:
        cos, sa, sb = cos_ref[...], sa_ref[...], sb_ref[...]

    def rot(t):
        if not rope:
            return t
        return t * cos + pltpu.roll(t, 96, 1) * sa + pltpu.roll(t, 32, 1) * sb

    scale = HEAD_DIM ** -0.5
    qn, kn = qn_ref[...], kn_ref[...]
    off = 0
    for hd in range(N_HEADS):
        t = proj[:, off + hd * HEAD_DIM: off + (hd + 1) * HEAD_DIM]
        qg_ref[:, hd * HEAD_DIM:(hd + 1) * HEAD_DIM] = (rot(_rms(t, qn)) * scale).astype(qg_ref.dtype)
    off += Q_W
    for hd in range(N_KV):
        t = proj[:, off + hd * HEAD_DIM: off + (hd + 1) * HEAD_DIM]
        kg_ref[:, hd * HEAD_DIM:(hd + 1) * HEAD_DIM] = rot(_rms(t, kn)).astype(kg_ref.dtype)
    off += KV_W
    vg_ref[...] = proj[:, off:off + KV_W].astype(vg_ref.dtype)
    off += KV_W
    for hd in range(N_HEADS):
        t = proj[:, off + hd * HEAD_DIM: off + (hd + 1) * HEAD_DIM]
        qw_ref[:, hd * HEAD_DIM:(hd + 1) * HEAD_DIM] = (rot(t) * scale).astype(qw_ref.dtype)
    off += Q_W
    for hd in range(N_KV):
        t = proj[:, off + hd * HEAD_DIM: off + (hd + 1) * HEAD_DIM]
        kw_ref[:, hd * HEAD_DIM:(hd + 1) * HEAD_DIM] = rot(t).astype(kw_ref.dtype)
    off += KV_W
    vw_ref[...] = proj[:, off:off + KV_W].astype(vw_ref.dtype)


def _qkv(x2d, mod, pre_g, w_in_b, qn, kn, cos, sa, sb, *, rope, seq, kv_dtype):
    t = x2d.shape[0]
    tiles_per_row = seq // TM
    full = lambda shape: pl.BlockSpec(shape, lambda i: (0,) * len(shape))
    tab = pl.BlockSpec((TM, HEAD_DIM), (lambda i: (i % tiles_per_row, 0)) if rope else (lambda i: (0, 0)))
    tok = lambda w: pl.BlockSpec((TM, w), lambda i: (i, 0))
    return pl.pallas_call(
        functools.partial(_qkv_kernel, rope=rope, tiles_per_row=tiles_per_row),
        grid=(t // TM,),
        in_specs=[tok(D_MODEL), full(mod.shape), full((1, D_MODEL)),
                  full((D_MODEL, QKV_WIDTH)), full((1, HEAD_DIM)), full((1, HEAD_DIM)),
                  tab, tab, tab],
        out_specs=[tok(Q_W), tok(KV_W), tok(KV_W), tok(Q_W), tok(KV_W), tok(KV_W)],
        out_shape=[jax.ShapeDtypeStruct((t, Q_W), BF16),
                   jax.ShapeDtypeStruct((t, KV_W), kv_dtype),
                   jax.ShapeDtypeStruct((t, KV_W), kv_dtype),
                   jax.ShapeDtypeStruct((t, Q_W), BF16),
                   jax.ShapeDtypeStruct((t, KV_W), kv_dtype),
                   jax.ShapeDtypeStruct((t, KV_W), kv_dtype)],
        compiler_params=_cparams(("arbitrary",)),
        name="qkv_rope" if rope else "qkv_ctx",
    )(x2d, mod, pre_g, w_in_b, qn, kn, cos, sa, sb)


def _softmax_pv(scores, values, sink):
    m = scores[0].max(axis=-1, keepdims=True)
    for s in scores[1:]:
        m = jnp.maximum(m, s.max(axis=-1, keepdims=True))
    if sink is not None:
        m = jnp.maximum(m, sink)
    den = jnp.exp(sink - m) if sink is not None else 0.0
    acc = None
    for s, v in zip(scores, values):
        p = jnp.exp(s - m)
        den = den + p.sum(axis=-1, keepdims=True)
        pv = jnp.dot(p.astype(BF16), v, preferred_element_type=F32)
        acc = pv if acc is None else acc + pv
    return acc / den


def _ctx_attn_kernel(sink_ref, qg_ref, kg_ref, vg_ref, qw_ref, kw_ref, vw_ref, o_ref):
    for mixer, (q_ref, k_ref, v_ref) in enumerate(((qg_ref, kg_ref, vg_ref), (qw_ref, kw_ref, vw_ref))):
        for n in range(N_KV):
            k = k_ref[:, n * HEAD_DIM:(n + 1) * HEAD_DIM].astype(BF16)
            v = v_ref[:, n * HEAD_DIM:(n + 1) * HEAD_DIM].astype(BF16)
            for g in range(GROUP):
                hd = n * GROUP + g
                q = q_ref[:, hd * HEAD_DIM:(hd + 1) * HEAD_DIM]
                sink = sink_ref[hd] if mixer == 1 else None
                o = _softmax_pv([_dot_nt(q, k)], [v], sink)
                c0 = mixer * Q_W + hd * HEAD_DIM
                o_ref[:, c0:c0 + HEAD_DIM] = o.astype(o_ref.dtype)


def _ctx_attention(sink, qg, kg, vg, qw, kw, vw, *, seq):
    t = qg.shape[0]
    blk = lambda w: pl.BlockSpec((seq, w), lambda b: (b, 0))
    return pl.pallas_call(
        _ctx_attn_kernel,
        grid=(t // seq,),
        in_specs=[pl.BlockSpec(memory_space=pltpu.SMEM),
                  blk(Q_W), blk(KV_W), blk(KV_W), blk(Q_W), blk(KV_W), blk(KV_W)],
        out_specs=blk(2 * Q_W),
        out_shape=jax.ShapeDtypeStruct((t, 2 * Q_W), BF16),
        compiler_params=_cparams(("arbitrary",)),
        name="attn_ctx",
    )(sink, qg, kg, vg, qw, kw, vw)


def _lat_attn_kernel(sink_ref, qg_ref, kg_ref, vg_ref, qw_ref, kw_ref, vw_ref,
                     cgk_ref, cgv_ref, cwk_ref, cwv_ref, o_ref, *, seq):
    qi = pl.program_id(1)
    span = TQ + 2 * WINDOW
    start = pl.multiple_of(jnp.clip(qi * TQ - WINDOW, 0, seq - span), WINDOW)
    qpos = qi * TQ + lax.broadcasted_iota(I32, (TQ, span), 0)
    kpos = start + lax.broadcasted_iota(I32, (TQ, span), 1)
    band_ok = jnp.abs(kpos - qpos) <= WINDOW
    for n in range(N_KV):
        cols = slice(n * HEAD_DIM, (n + 1) * HEAD_DIM)
        kc, vc = cgk_ref[:, cols].astype(BF16), cgv_ref[:, cols].astype(BF16)
        kl, vl = kg_ref[:, cols], vg_ref[:, cols]
        for g in range(GROUP):
            hd = n * GROUP + g
            q = qg_ref[:, hd * HEAD_DIM:(hd + 1) * HEAD_DIM]
            o = _softmax_pv([_dot_nt(q, kc), _dot_nt(q, kl)], [vc, vl], None)
            o_ref[:, hd * HEAD_DIM:(hd + 1) * HEAD_DIM] = o.astype(o_ref.dtype)
        kc, vc = cwk_ref[:, cols].astype(BF16), cwv_ref[:, cols].astype(BF16)
        kl, vl = kw_ref[pl.ds(start, span), cols], vw_ref[pl.ds(start, span), cols]
        for g in range(GROUP):
            hd = n * GROUP + g
            q = qw_ref[:, hd * HEAD_DIM:(hd + 1) * HEAD_DIM]
            band = jnp.where(band_ok, _dot_nt(q, kl), NEG)
            o = _softmax_pv([_dot_nt(q, kc), band], [vc, vl], sink_ref[hd])
            c0 = Q_W + hd * HEAD_DIM
            o_ref[:, c0:c0 + HEAD_DIM] = o.astype(o_ref.dtype)


def _lat_attention(sink, qg, kg, vg, qw, kw, vw, cgk, cgv, cwk, cwv, *, seq, past):
    t = qg.shape[0]
    nq = seq // TQ
    qblk = pl.BlockSpec((TQ, Q_W), lambda b, i: (b * nq + i, 0))
    kblk = pl.BlockSpec((seq, KV_W), lambda b, i: (b, 0))
    cblk = pl.BlockSpec((past, KV_W), lambda b, i: (b, 0))
    return pl.pallas_call(
        functools.partial(_lat_attn_kernel, seq=seq),
        grid=(t // seq, nq),
        in_specs=[pl.BlockSpec(memory_space=pltpu.SMEM),
                  qblk, kblk, kblk, qblk, kblk, kblk, cblk, cblk, cblk, cblk],
        out_specs=pl.BlockSpec((TQ, 2 * Q_W), lambda b, i: (b * nq + i, 0)),
        out_shape=jax.ShapeDtypeStruct((t, 2 * Q_W), BF16),
        compiler_params=_cparams(("arbitrary", "arbitrary")),
        name="attn_lat",
    )(sink, qg, kg, vg, qw, kw, vw, cgk, cgv, cwk, cwv)


def _route(sel, scores, tm):
    per = N_EXPERTS // N_GROUPS
    sel3 = sel.reshape(N_GROUPS, per, tm)
    sc3 = scores.reshape(N_GROUPS, per, tm)
    member = lax.broadcasted_iota(I32, (N_GROUPS, per, tm), 1)
    m1 = sel3.max(axis=1, keepdims=True)
    first = jnp.where(sel3 == m1, member, per).min(axis=1, keepdims=True)
    m2 = jnp.where(member == first, -jnp.inf, sel3).max(axis=1, keepdims=True)
    gs = m1 + m2

    gid = lax.broadcasted_iota(I32, (N_GROUPS, 1, tm), 0)
    chosen = jnp.zeros((N_GROUPS, 1, tm), F32)
    for _ in range(TOPK_GROUPS):
        gm = gs.max(axis=0, keepdims=True)
        gfirst = jnp.where(gs == gm, gid, N_GROUPS).min(axis=0, keepdims=True)
        hit = gid == gfirst
        chosen = jnp.where(hit, 1.0, chosen)
        gs = jnp.where(hit, -jnp.inf, gs)

    eid = lax.broadcasted_iota(I32, (N_GROUPS, per, tm), 0) * per + member
    masked = jnp.where(chosen > 0.0, sel3, -jnp.inf)
    ids, raw = [], []
    for _ in range(TOP_K):
        mx = masked.max(axis=1, keepdims=True).max(axis=0, keepdims=True)
        efirst = jnp.where(masked == mx, eid, N_EXPERTS).min(axis=1, keepdims=True).min(axis=0, keepdims=True)
        hit = eid == efirst
        ids.append(efirst.reshape(1, tm))
        raw.append(jnp.where(hit, sc3, 0.0).sum(axis=1, keepdims=True).sum(axis=0, keepdims=True).reshape(1, tm))
        masked = jnp.where(hit, -jnp.inf, masked)
    ids = jnp.concatenate(ids, axis=0)
    raw = jnp.concatenate(raw, axis=0)
    gates = raw / raw.sum(axis=0, keepdims=True) * ROUTED_SCALE
    return ids, gates


def _post_attn_kernel(op_ref, os_ref, xp_ref, xs_ref, mod_ref, w_ref, pg_ref, fg_ref,
                      wrh_ref, wrl_ref, rb_ref,
                      y_ref, h_ref, ids_ref, gates_ref, *, ctx_tiles, tiles_per_row):
    i = pl.program_id(0)
    is_ctx = i < ctx_tiles
    row = jnp.where(is_ctx, 0, 1 + (i - ctx_tiles) // tiles_per_row)
    o = jnp.where(is_ctx, op_ref[...], os_ref[...])
    x = jnp.where(is_ctx, xp_ref[...], xs_ref[...])
    a = jnp.dot(o, w_ref[...], preferred_element_type=F32)
    y = x + _mod_chunk(mod_ref, row, 2) * _rms(a, pg_ref[...])
    y_ref[...] = y
    h = _rms(y, fg_ref[...]) * (1.0 + _mod_chunk(mod_ref, row, 4)) + _mod_chunk(mod_ref, row, 3)
    _store_rows(h_ref, h.shape[0], _pack_bf16_pair(h[:, :HALF], h[:, HALF:]))
    h_hi = h.astype(BF16)
    h_lo = (h - h_hi.astype(F32)).astype(BF16)
    wh, wl = wrh_ref[...], wrl_ref[...]
    logits = _dot_nt(wh, h_hi) + (_dot_nt(wl, h_hi) + _dot_nt(wh, h_lo))
    scores = 1.0 / (1.0 + jnp.exp(-logits))
    ids, gates = _route(scores + rb_ref[...], scores, logits.shape[1])
    ids_ref[...] = ids
    gates_ref[...] = gates


def _post_attn(o_p, o_s, x_p, x_s, mod, w_out_b, post_g, ffn_g, wr_hi, wr_lo, rbias, *, seq):
    tp, ts = x_p.shape[0], x_s.shape[0]
    t = tp + ts
    ctx_tiles = tp // TM
    full = lambda shape: pl.BlockSpec(shape, lambda i: (0,) * len(shape))
    ctx = lambda w: pl.BlockSpec((TM, w), lambda i: (jnp.minimum(i, ctx_tiles - 1), 0))
    lat = lambda w: pl.BlockSpec((TM, w), lambda i: (jnp.maximum(i - ctx_tiles, 0), 0))
    tok = lambda w: pl.BlockSpec((TM, w), lambda i: (i, 0))
    lane = pl.BlockSpec((TOP_K, TM), lambda i: (0, i))
    return pl.pallas_call(
        functools.partial(_post_attn_kernel, ctx_tiles=ctx_tiles, tiles_per_row=seq // TM),
        grid=(t // TM,),
        in_specs=[ctx(D_MODEL), lat(D_MODEL), ctx(D_MODEL), lat(D_MODEL), full(mod.shape),
                  full((D_MODEL, D_MODEL)), full((1, D_MODEL)), full((1, D_MODEL)),
                  full((N_EXPERTS, D_MODEL)), full((N_EXPERTS, D_MODEL)), full((N_EXPERTS, 1))],
        out_specs=[tok(D_MODEL), pl.BlockSpec((TM * CHUNKS, LANES), lambda i: (i, 0)), lane, lane],
        out_shape=[jax.ShapeDtypeStruct((t, D_MODEL), F32),
                   jax.ShapeDtypeStruct((t * CHUNKS, LANES), U32),
                   jax.ShapeDtypeStruct((TOP_K, t), I32),
                   jax.ShapeDtypeStruct((TOP_K, t), F32)],
        compiler_params=_cparams(("arbitrary",)),
        name="post_attn_router",
    )(o_p, o_s, x_p, x_s, mod, w_out_b, post_g, ffn_g, wr_hi, wr_lo, rbias)


def _stage_indices(idx_vmem, idx_smem, sem):
    cp = pltpu.make_async_copy(idx_vmem.at[0, 0], idx_smem, sem)
    cp.start()
    cp.wait()


def _issue_row_dmas(idx_smem, copy_of):
    def slot_body(j, carry):
        def chunk(c, carry):
            for u in range(ISSUE_UNROLL):
                tok = c * ISSUE_UNROLL + u
                copy_of(j, tok, pl.multiple_of(idx_smem[j * TM + tok], CHUNKS)).start()
            return carry
        return lax.fori_loop(0, TM // ISSUE_UNROLL, chunk, carry)
    lax.fori_loop(0, TOP_K, slot_body, 0)


def _dispatch_kernel(idx_ref, h_ref, xs_hbm, idx_smem, isem, dsem):
    _stage_indices(idx_ref, idx_smem, isem)

    def copy_of(j, tok, dst):
        src = pl.multiple_of(tok * CHUNKS, CHUNKS)
        return pltpu.make_async_copy(h_ref.at[pl.ds(src, CHUNKS)], xs_hbm.at[pl.ds(dst, CHUNKS)], dsem)

    _issue_row_dmas(idx_smem, copy_of)
    for _ in range(TOP_K):
        pltpu.make_async_copy(h_ref, xs_hbm.at[pl.ds(0, TM * CHUNKS)], dsem).wait()


def _dispatch(dest_tiles, h_packed):
    n_tiles = dest_tiles.shape[0]
    rows = n_tiles * TM * TOP_K
    return pl.pallas_call(
        _dispatch_kernel,
        grid=(n_tiles,),
        in_specs=[pl.BlockSpec((1, 1, TOP_K * TM), lambda i: (i, 0, 0)),
                  pl.BlockSpec((TM * CHUNKS, LANES), lambda i: (i, 0))],
        out_specs=pl.BlockSpec(memory_space=pl.ANY),
        out_shape=jax.ShapeDtypeStruct((rows * CHUNKS, LANES), U32),
        scratch_shapes=[pltpu.SMEM((TOP_K * TM,), I32),
                        pltpu.SemaphoreType.DMA,
                        pltpu.SemaphoreType.DMA],
        compiler_params=_cparams(("arbitrary",)),
        name="dispatch",
    )(dest_tiles, h_packed)


def _experts_kernel(vb_ref, ve_ref, vlo_ref, vhi_ref, vfirst_ref, nvis_ref,
                    x_ref, wg_ref, wu_ref, wd_ref, out_ref, wgb, wub, wdb):
    v = pl.program_id(0)

    @pl.when(v < nvis_ref[0])
    def _():
        prev = ve_ref[jnp.maximum(v - 1, 0)]

        @pl.when((v == 0) | (ve_ref[v] != prev))
        def _():
            wgb[...] = wg_ref[0].astype(BF16)
            wub[...] = wu_ref[0].astype(BF16)
            wdb[...] = wd_ref[0].astype(BF16)

        @pl.when(vfirst_ref[v] == 1)
        def _():
            out_ref[...] = jnp.zeros_like(out_ref)

        y = _swiglu_packed(_load_rows(x_ref, TE), wgb, wub, wdb)
        rows = lax.broadcasted_iota(I32, (TE, 1), 0)
        mine = (rows >= vlo_ref[v]) & (rows < vhi_ref[v])
        y = jnp.where(mine, _pack_bf16_pair(y[:, :HALF], y[:, HALF:]), _load_rows(out_ref, TE))
        _store_rows(out_ref, TE, y)


def _experts(visits, xs, w_gate_e, w_up_e, w_down_e):
    n_visits = visits[0].shape[0]
    n_rows = xs.shape[0] // CHUNKS
    wspec = lambda shape: pl.BlockSpec((1,) + shape, lambda v, vb, ve, *_: (ve[v], 0, 0))
    rows = pl.BlockSpec((TE * CHUNKS, LANES), lambda v, vb, *_: (vb[v], 0))
    grid_spec = pltpu.PrefetchScalarGridSpec(
        num_scalar_prefetch=6,
        grid=(n_visits,),
        in_specs=[rows, wspec((D_MODEL, D_EXPERT)), wspec((D_MODEL, D_EXPERT)), wspec((D_EXPERT, D_MODEL))],
        out_specs=rows,
        scratch_shapes=[pltpu.VMEM((D_MODEL, D_EXPERT), BF16),
                        pltpu.VMEM((D_MODEL, D_EXPERT), BF16),
                        pltpu.VMEM((D_EXPERT, D_MODEL), BF16)])
    return pl.pallas_call(
        _experts_kernel,
        grid_spec=grid_spec,
        out_shape=jax.ShapeDtypeStruct((n_rows * CHUNKS, LANES), U32),
        compiler_params=_cparams(("arbitrary",)),
        name="experts",
    )(*visits, xs, w_gate_e, w_up_e, w_down_e)


def _ffn_out_kernel(cur_ref, nxt_ref, yb_hbm, gates_ref, h_ref, y1_ref, mod_ref, fg_ref,
                    wg_ref, wu_ref, wd_ref, out_ref, idx_smem, gbuf, isem, gsem,
                    *, n_tiles, tiles_per_row, mod_row0):
    i = pl.program_id(0)
    slot = i % 2
    n_idx = TOP_K * TM

    def gather(idx_vmem, s):
        _stage_indices(idx_vmem, idx_smem, isem)
        _issue_row_dmas(idx_smem, lambda j, tok, src: pltpu.make_async_copy(
            yb_hbm.at[pl.ds(src, CHUNKS)],
            gbuf.at[s, pl.ds(pl.multiple_of((j * TM + tok) * CHUNKS, CHUNKS), CHUNKS)], gsem.at[s]))

    @pl.when(i == 0)
    def _():
        gather(cur_ref, 0)

    @pl.when(i + 1 < n_tiles)
    def _():
        gather(nxt_ref, 1 - slot)

    shared = _swiglu_packed(_load_rows(h_ref, TM), wg_ref, wu_ref, wd_ref)

    pltpu.make_async_copy(yb_hbm.at[pl.ds(0, n_idx * CHUNKS)], gbuf.at[slot], gsem.at[slot]).wait()
    g = gates_ref[...]
    acc_lo = shared[:, :HALF]
    acc_hi = shared[:, HALF:]
    for j in range(TOP_K):
        r_lo, r_hi = _unpack_bf16_pair(_load_rows(gbuf.at[slot], TM, first=j * TM))
        acc_lo = acc_lo + g[:, j:j + 1] * r_lo
        acc_hi = acc_hi + g[:, j:j + 1] * r_hi
    ffn = jnp.concatenate([acc_lo, acc_hi], axis=-1)
    row = mod_row0 + (i // tiles_per_row if tiles_per_row else 0)
    out_ref[...] = y1_ref[...] + _mod_chunk(mod_ref, row, 5) * _rms(ffn, fg_ref[...])


def _ffn_out(dest_tiles, yb, gates_t, h_packed, y1, mod, post_g, wg_b, wu_b, wd_b,
             *, tile0, n_tiles, tiles_per_row, mod_row0):
    full = lambda shape: pl.BlockSpec(shape, lambda i: (0,) * len(shape))
    tok = lambda w: pl.BlockSpec((TM, w), lambda i: (tile0 + i, 0))
    last = tile0 + n_tiles - 1
    idx = lambda step: pl.BlockSpec((1, 1, TOP_K * TM),
                                    lambda i: (jnp.minimum(tile0 + i + step, last), 0, 0))
    return pl.pallas_call(
        functools.partial(_ffn_out_kernel, n_tiles=n_tiles,
                          tiles_per_row=tiles_per_row, mod_row0=mod_row0),
        grid=(n_tiles,),
        in_specs=[idx(0), idx(1), pl.BlockSpec(memory_space=pl.ANY),
                  tok(TOP_K), pl.BlockSpec((TM * CHUNKS, LANES), lambda i: (tile0 + i, 0)),
                  tok(D_MODEL), full(mod.shape), full((1, D_MODEL)),
                  full((D_MODEL, D_EXPERT)), full((D_MODEL, D_EXPERT)), full((D_EXPERT, D_MODEL))],
        out_specs=pl.BlockSpec((TM, D_MODEL), lambda i: (i, 0)),
        out_shape=jax.ShapeDtypeStruct((n_tiles * TM, D_MODEL), F32),
        scratch_shapes=[pltpu.SMEM((TOP_K * TM,), I32),
                        pltpu.VMEM((2, TOP_K * TM * CHUNKS, LANES), U32),
                        pltpu.SemaphoreType.DMA,
                        pltpu.SemaphoreType.DMA((2,))],
        compiler_params=_cparams(("arbitrary",)),
        name="ffn_out",
    )(dest_tiles, dest_tiles, yb, gates_t, h_packed, y1, mod, post_g, wg_b, wu_b, wd_b)


def _routing_tables(ids):
    t = ids.shape[1]
    n_blocks = t * TOP_K // TE
    n_visits = n_blocks + N_EXPERTS - 1
    ids_t = ids.T
    onehot = (ids_t[:, :, None] == jnp.arange(N_EXPERTS, dtype=I32)).sum(axis=1).astype(I32)
    before = jnp.cumsum(onehot, axis=0) - onehot
    counts = onehot.sum(axis=0)
    end = jnp.cumsum(counts)
    start = end - counts
    dest = start[ids_t] + jnp.take_along_axis(before, ids_t, axis=1)

    first_blk = start // TE
    n_vis_e = jnp.where(counts > 0, (end - 1) // TE - first_blk + 1, 0)
    vend = jnp.cumsum(n_vis_e)
    vstart = vend - n_vis_e
    n_vis = vend[-1]
    v = jnp.arange(n_visits, dtype=I32)
    valid = v < n_vis
    ve = jnp.minimum((vend[None, :] <= v[:, None]).sum(axis=1), N_EXPERTS - 1).astype(I32)
    ve = jnp.where(valid, ve, ve[n_vis - 1])
    vb = jnp.where(valid, first_blk[ve] + v - vstart[ve], n_blocks - 1).astype(I32)
    lo = jnp.where(valid, jnp.clip(start[ve] - vb * TE, 0, TE), 0).astype(I32)
    hi = jnp.where(valid, jnp.clip(end[ve] - vb * TE, 0, TE), 0).astype(I32)
    vfirst = (valid & ((v == 0) | (vb != jnp.roll(vb, 1)))).astype(I32)
    return dest.astype(I32), (vb, ve, lo, hi, vfirst, n_vis.reshape(1).astype(I32))


def _rope_tables(n_tokens):
    rows = n_tokens // GRID_W
    row = jnp.repeat(jnp.arange(rows, dtype=F32), GRID_W)
    col = jnp.tile(jnp.arange(GRID_W, dtype=F32), rows)
    inv_freq = ROPE_THETA ** (-jnp.arange(0, AXIS_DIM, 2, dtype=F32) / AXIS_DIM)
    ang_r = row[:, None] * inv_freq
    ang_c = col[:, None] * inv_freq
    ang = jnp.concatenate([ang_r, ang_r, ang_c, ang_c], axis=-1)
    cos, sin = jnp.cos(ang), jnp.sin(ang)
    first = (jnp.arange(HEAD_DIM) % AXIS_DIM) < (AXIS_DIM // 2)
    return cos, jnp.where(first, -sin, 0.0), jnp.where(first, 0.0, sin)


def kernel(x_prompt, x_sample, cache_glob_k, cache_glob_v, cache_win_k, cache_win_v, c, c_ctx,
           w_ada, b_ada, attn_pre_g, attn_post_g, w_in, q_norm_g, k_norm_g, sink_logit, w_out,
           ffn_pre_g, ffn_post_g, w_router, router_bias, w_gate_e, w_up_e, w_down_e,
           w_gate_s, w_up_s, w_down_s):
    batch, seq, d = x_prompt.shape
    dec_batch, dec_seq, _ = x_sample.shape
    past = cache_glob_k.shape[2]
    tp, ts = batch * seq, dec_batch * dec_seq
    t = tp + ts
    l = 0

    xp = x_prompt.reshape(tp, d)
    xs = x_sample.reshape(ts, d)
    cond = jnp.concatenate([c_ctx[None, :], c, jnp.zeros((MOD_ROWS - 1 - dec_batch, d), F32)], axis=0)
    mod = _adaln(cond, w_ada[l], b_ada[l][None, :])

    w_in_b = w_in[l].astype(BF16)
    pre_g = attn_pre_g[l][None, :]
    qn, kn = q_norm_g[l][None, :], k_norm_g[l][None, :]
    cos, sa, sb = _rope_tables(dec_seq)
    sink = sink_logit[l]

    qg, kg, vg, qw, kw, vw = _qkv(xp, mod, pre_g, w_in_b, qn, kn, cos, sa, sb,
                                  rope=False, seq=seq, kv_dtype=F32)
    o_p = _ctx_attention(sink, qg, kg, vg, qw, kw, vw, seq=seq)
    new_kv = [a.reshape(batch, 1, seq, N_KV, HEAD_DIM) for a in (kg, vg, kw, vw)]

    qg, kg, vg, qw, kw, vw = _qkv(xs, mod, pre_g, w_in_b, qn, kn, cos, sa, sb,
                                  rope=True, seq=dec_seq, kv_dtype=BF16)
    caches = [a[:, l].reshape(dec_batch * past, KV_W)
              for a in (cache_glob_k, cache_glob_v, cache_win_k, cache_win_v)]
    o_s = _lat_attention(sink, qg, kg, vg, qw, kw, vw, *caches, seq=dec_seq, past=past)

    wr_t = w_router[l].T
    wr_hi = wr_t.astype(BF16)
    wr_lo = (wr_t - wr_hi.astype(F32)).astype(BF16)
    y1, h_packed, ids, gates = _post_attn(
        o_p, o_s, xp, xs, mod, w_out[l].astype(BF16), attn_post_g[l][None, :], ffn_pre_g[l][None, :],
        wr_hi, wr_lo, router_bias[l][:, None], seq=dec_seq)

    dest, visits = _routing_tables(ids)
    dest_tiles = (dest * CHUNKS).reshape(t // TM, TM, TOP_K).transpose(0, 2, 1).reshape(t // TM, 1, TOP_K * TM)
    xs_rows = _dispatch(dest_tiles, h_packed)
    yb = _experts(visits, xs_rows, w_gate_e[l], w_up_e[l], w_down_e[l])

    gates_t = gates.T
    shared = (w_gate_s[l].astype(BF16), w_up_s[l].astype(BF16), w_down_s[l].astype(BF16))
    post_g = ffn_post_g[l][None, :]
    y_p = _ffn_out(dest_tiles, yb, gates_t, h_packed, y1, mod, post_g, *shared,
                   tile0=0, n_tiles=tp // TM, tiles_per_row=0, mod_row0=0)
    y_s = _ffn_out(dest_tiles, yb, gates_t, h_packed, y1, mod, post_g, *shared,
                   tile0=tp // TM, n_tiles=ts // TM, tiles_per_row=dec_seq // TM, mod_row0=1)
    return (y_p.reshape(batch, seq, d), y_s.reshape(dec_batch, dec_seq, d), *new_kv)
```

```python
import functools

import jax
import jax.numpy as jnp
from jax import lax
from jax.experimental import pallas as pl
from jax.experimental.pallas import tpu as pltpu

F32 = jnp.float32
BF16 = jnp.bfloat16
U32 = jnp.uint32
I32 = jnp.int32

D_MODEL = 2048
GRID_W = 64
HEAD_DIM = 128
AXIS_DIM = HEAD_DIM // 2
N_HEADS = 8
N_KV = 2
GROUP = N_HEADS // N_KV
Q_W = N_HEADS * HEAD_DIM
KV_W = N_KV * HEAD_DIM
QKV_WIDTH = 2 * Q_W + 4 * KV_W
WINDOW = 128
ROPE_THETA = 10000.0
NORM_EPS = 1e-6
N_EXPERTS = 64
N_GROUPS = 8
TOPK_GROUPS = 4
TOP_K = 8
TOP_K_BITS = 3
D_EXPERT = 512
ROUTED_SCALE = 2.5

HALF = D_MODEL // 2
LANES = 128
CHUNKS = HALF // LANES
MOD_ROWS = 8
VMEM_LIMIT = 56 * 1024 * 1024

TM = 256
TQ = 256
TE = 256
NEG = -1e30
ISSUE_UNROLL = 32


def _cparams(sem):
    return pltpu.CompilerParams(dimension_semantics=sem, vmem_limit_bytes=VMEM_LIMIT)


def _pack_bf16_pair(lo, hi):
    lo_b = lax.bitcast_convert_type(lo.astype(BF16).astype(F32), U32)
    hi_b = lax.bitcast_convert_type(hi.astype(BF16).astype(F32), U32)
    return (hi_b & jnp.uint32(0xFFFF0000)) | (lo_b >> 16)


def _unpack_bf16_pair(p):
    lo = lax.bitcast_convert_type(p << 16, F32)
    hi = lax.bitcast_convert_type(p & jnp.uint32(0xFFFF0000), F32)
    return lo, hi


def _store_rows(ref, n, packed):
    for c in range(CHUNKS):
        ref[pl.ds(c, n, stride=CHUNKS), :] = packed[:, c * LANES:(c + 1) * LANES]


def _load_rows(ref, n, first=0):
    return jnp.concatenate(
        [ref[pl.ds(first * CHUNKS + c, n, stride=CHUNKS), :] for c in range(CHUNKS)], axis=-1)


def _rms(x, gain):
    r = lax.rsqrt(jnp.mean(x * x, axis=-1, keepdims=True) + NORM_EPS)
    return x * r * gain


def _dot_nt(a, b):
    return lax.dot_general(a, b, (((1,), (1,)), ((), ())), preferred_element_type=F32)


def _swiglu_packed(p, wg, wu, wd):
    lo, hi = _unpack_bf16_pair(p)
    lo, hi = lo.astype(BF16), hi.astype(BF16)
    gate = (jnp.dot(lo, wg[:HALF], preferred_element_type=F32)
            + jnp.dot(hi, wg[HALF:], preferred_element_type=F32))
    up = (jnp.dot(lo, wu[:HALF], preferred_element_type=F32)
          + jnp.dot(hi, wu[HALF:], preferred_element_type=F32))
    act = (gate / (1.0 + jnp.exp(-gate)) * up).astype(BF16)
    return jnp.dot(act, wd[...], preferred_element_type=F32)


def _adaln_kernel(cond_ref, w_ref, b_ref, o_ref):
    c = cond_ref[...]
    s = (c / (1.0 + jnp.exp(-c))).astype(BF16)
    o_ref[...] = jnp.dot(s, w_ref[...].astype(BF16), preferred_element_type=F32) + b_ref[...]


def _adaln(cond, w_ada, b_ada):
    n = w_ada.shape[1]
    tn = 1024
    return pl.pallas_call(
        _adaln_kernel,
        grid=(n // tn,),
        in_specs=[pl.BlockSpec((MOD_ROWS, D_MODEL), lambda j: (0, 0)),
                  pl.BlockSpec((D_MODEL, tn), lambda j: (0, j)),
                  pl.BlockSpec((1, tn), lambda j: (0, j))],
        out_specs=pl.BlockSpec((MOD_ROWS, tn), lambda j: (0, j)),
        out_shape=jax.ShapeDtypeStruct((MOD_ROWS, n), F32),
        compiler_params=_cparams(("arbitrary",)),
        name="adaln",
    )(cond, w_ada, b_ada)


def _mod_chunk(mod_ref, row, k):
    return mod_ref[pl.ds(row, 1), k * D_MODEL:(k + 1) * D_MODEL]


def _qkv_kernel(x_ref, mod_ref, g_ref, w_ref, qn_ref, kn_ref, cos_ref, sa_ref, sb_ref,
                qg_ref, kg_ref, vg_ref, qw_ref, kw_ref, vw_ref, *, rope, tiles_per_row):
    i = pl.program_id(0)
    row = (1 + i // tiles_per_row) if rope else 0
    x = x_ref[...]
    h = _rms(x, g_ref[...]) * (1.0 + _mod_chunk(mod_ref, row, 1)) + _mod_chunk(mod_ref, row, 0)
    proj = jnp.dot(h.astype(BF16), w_ref[...], preferred_element_type=F32)

    if rope:
        cos, sa, sb = cos_ref[...], sa_ref[...], sb_ref[...]

    def rot(t):
        if not rope:
            return t
        return t * cos + pltpu.roll(t, 96, 1) * sa + pltpu.roll(t, 32, 1) * sb

    scale = HEAD_DIM ** -0.5
    qn, kn = qn_ref[...], kn_ref[...]
    off = 0
    for hd in range(N_HEADS):
        t = proj[:, off + hd * HEAD_DIM: off + (hd + 1) * HEAD_DIM]
        qg_ref[:, hd * HEAD_DIM:(hd + 1) * HEAD_DIM] = (rot(_rms(t, qn)) * scale).astype(qg_ref.dtype)
    off += Q_W
    for hd in range(N_KV):
        t = proj[:, off + hd * HEAD_DIM: off + (hd + 1) * HEAD_DIM]
        kg_ref[:, hd * HEAD_DIM:(hd + 1) * HEAD_DIM] = rot(_rms(t, kn)).astype(kg_ref.dtype)
    off += KV_W
    vg_ref[...] = proj[:, off:off + KV_W].astype(vg_ref.dtype)
    off += KV_W
    for hd in range(N_HEADS):
        t = proj[:, off + hd * HEAD_DIM: off + (hd + 1) * HEAD_DIM]
        qw_ref[:, hd * HEAD_DIM:(hd + 1) * HEAD_DIM] = (rot(t) * scale).astype(qw_ref.dtype)
    off += Q_W
    for hd in range(N_KV):
        t = proj[:, off + hd * HEAD_DIM: off + (hd + 1) * HEAD_DIM]
        kw_ref[:, hd * HEAD_DIM:(hd + 1) * HEAD_DIM] = rot(t).astype(kw_ref.dtype)
    off += KV_W
    vw_ref[...] = proj[:, off:off + KV_W].astype(vw_ref.dtype)


def _qkv(x2d, mod, pre_g, w_in_b, qn, kn, cos, sa, sb, *, rope, seq, kv_dtype):
    t = x2d.shape[0]
    tiles_per_row = seq // TM
    full = lambda shape: pl.BlockSpec(shape, lambda i: (0,) * len(shape))
    tab = pl.BlockSpec((TM, HEAD_DIM), (lambda i: (i % tiles_per_row, 0)) if rope else (lambda i: (0, 0)))
    tok = lambda w: pl.BlockSpec((TM, w), lambda i: (i, 0))
    return pl.pallas_call(
        functools.partial(_qkv_kernel, rope=rope, tiles_per_row=tiles_per_row),
        grid=(t // TM,),
        in_specs=[tok(D_MODEL), full(mod.shape), full((1, D_MODEL)),
                  full((D_MODEL, QKV_WIDTH)), full((1, HEAD_DIM)), full((1, HEAD_DIM)),
                  tab, tab, tab],
        out_specs=[tok(Q_W), tok(KV_W), tok(KV_W), tok(Q_W), tok(KV_W), tok(KV_W)],
        out_shape=[jax.ShapeDtypeStruct((t, Q_W), BF16),
                   jax.ShapeDtypeStruct((t, KV_W), kv_dtype),
                   jax.ShapeDtypeStruct((t, KV_W), kv_dtype),
                   jax.ShapeDtypeStruct((t, Q_W), BF16),
                   jax.ShapeDtypeStruct((t, KV_W), kv_dtype),
                   jax.ShapeDtypeStruct((t, KV_W), kv_dtype)],
        compiler_params=_cparams(("arbitrary",)),
        name="qkv_rope" if rope else "qkv_ctx",
    )(x2d, mod, pre_g, w_in_b, qn, kn, cos, sa, sb)


def _softmax_pv(scores, values, sink):
    m = scores[0].max(axis=-1, keepdims=True)
    for s in scores[1:]:
        m = jnp.maximum(m, s.max(axis=-1, keepdims=True))
    if sink is not None:
        m = jnp.maximum(m, sink)
    den = jnp.exp(sink - m) if sink is not None else 0.0
    acc = None
    for s, v in zip(scores, values):
        p = jnp.exp(s - m)
        den = den + p.sum(axis=-1, keepdims=True)
        pv = jnp.dot(p.astype(BF16), v, preferred_element_type=F32)
        acc = pv if acc is None else acc + pv
    return acc / den


def _ctx_attn_kernel(sink_ref, qg_ref, kg_ref, vg_ref, qw_ref, kw_ref, vw_ref, o_ref):
    for mixer, (q_ref, k_ref, v_ref) in enumerate(((qg_ref, kg_ref, vg_ref), (qw_ref, kw_ref, vw_ref))):
        for n in range(N_KV):
            k = k_ref[:, n * HEAD_DIM:(n + 1) * HEAD_DIM].astype(BF16)
            v = v_ref[:, n * HEAD_DIM:(n + 1) * HEAD_DIM].astype(BF16)
            for g in range(GROUP):
                hd = n * GROUP + g
                q = q_ref[:, hd * HEAD_DIM:(hd + 1) * HEAD_DIM]
                sink = sink_ref[hd] if mixer == 1 else None
                o = _softmax_pv([_dot_nt(q, k)], [v], sink)
                c0 = mixer * Q_W + hd * HEAD_DIM
                o_ref[:, c0:c0 + HEAD_DIM] = o.astype(o_ref.dtype)


def _ctx_attention(sink, qg, kg, vg, qw, kw, vw, *, seq):
    t = qg.shape[0]
    blk = lambda w: pl.BlockSpec((seq, w), lambda b: (b, 0))
    return pl.pallas_call(
        _ctx_attn_kernel,
        grid=(t // seq,),
        in_specs=[pl.BlockSpec(memory_space=pltpu.SMEM),
                  blk(Q_W), blk(KV_W), blk(KV_W), blk(Q_W), blk(KV_W), blk(KV_W)],
        out_specs=blk(2 * Q_W),
        out_shape=jax.ShapeDtypeStruct((t, 2 * Q_W), BF16),
        compiler_params=_cparams(("arbitrary",)),
        name="attn_ctx",
    )(sink, qg, kg, vg, qw, kw, vw)


def _lat_attn_kernel(sink_ref, qg_ref, kg_ref, vg_ref, qw_ref, kw_ref, vw_ref,
                     cgk_ref, cgv_ref, cwk_ref, cwv_ref, o_ref, *, seq):
    qi = pl.program_id(1)
    span = TQ + 2 * WINDOW
    start = pl.multiple_of(jnp.clip(qi * TQ - WINDOW, 0, seq - span), WINDOW)
    qpos = qi * TQ + lax.broadcasted_iota(I32, (TQ, span), 0)
    kpos = start + lax.broadcasted_iota(I32, (TQ, span), 1)
    band_ok = jnp.abs(kpos - qpos) <= WINDOW
    for n in range(N_KV):
        cols = slice(n * HEAD_DIM, (n + 1) * HEAD_DIM)
        kc, vc = cgk_ref[:, cols].astype(BF16), cgv_ref[:, cols].astype(BF16)
        kl, vl = kg_ref[:, cols], vg_ref[:, cols]
        for g in range(GROUP):
            hd = n * GROUP + g
            q = qg_ref[:, hd * HEAD_DIM:(hd + 1) * HEAD_DIM]
            o = _softmax_pv([_dot_nt(q, kc), _dot_nt(q, kl)], [vc, vl], None)
            o_ref[:, hd * HEAD_DIM:(hd + 1) * HEAD_DIM] = o.astype(o_ref.dtype)
        kc, vc = cwk_ref[:, cols].astype(BF16), cwv_ref[:, cols].astype(BF16)
        kl, vl = kw_ref[pl.ds(start, span), cols], vw_ref[pl.ds(start, span), cols]
        for g in range(GROUP):
            hd = n * GROUP + g
            q = qw_ref[:, hd * HEAD_DIM:(hd + 1) * HEAD_DIM]
            band = jnp.where(band_ok, _dot_nt(q, kl), NEG)
            o = _softmax_pv([_dot_nt(q, kc), band], [vc, vl], sink_ref[hd])
            c0 = Q_W + hd * HEAD_DIM
            o_ref[:, c0:c0 + HEAD_DIM] = o.astype(o_ref.dtype)


def _lat_attention(sink, qg, kg, vg, qw, kw, vw, cgk, cgv, cwk, cwv, *, seq, past):
    t = qg.shape[0]
    nq = seq // TQ
    qblk = pl.BlockSpec((TQ, Q_W), lambda b, i: (b * nq + i, 0))
    kblk = pl.BlockSpec((seq, KV_W), lambda b, i: (b, 0))
    cblk = pl.BlockSpec((past, KV_W), lambda b, i: (b, 0))
    return pl.pallas_call(
        functools.partial(_lat_attn_kernel, seq=seq),
        grid=(t // seq, nq),
        in_specs=[pl.BlockSpec(memory_space=pltpu.SMEM),
                  qblk, kblk, kblk, qblk, kblk, kblk, cblk, cblk, cblk, cblk],
        out_specs=pl.BlockSpec((TQ, 2 * Q_W), lambda b, i: (b * nq + i, 0)),
        out_shape=jax.ShapeDtypeStruct((t, 2 * Q_W), BF16),
        compiler_params=_cparams(("arbitrary", "arbitrary")),
        name="attn_lat",
    )(sink, qg, kg, vg, qw, kw, vw, cgk, cgv, cwk, cwv)


def _route(sel, scores, tm):
    per = N_EXPERTS // N_GROUPS
    sel3 = sel.reshape(N_GROUPS, per, tm)
    sc3 = scores.reshape(N_GROUPS, per, tm)
    member = lax.broadcasted_iota(I32, (N_GROUPS, per, tm), 1)
    m1 = sel3.max(axis=1, keepdims=True)
    first = jnp.where(sel3 == m1, member, per).min(axis=1, keepdims=True)
    m2 = jnp.where(member == first, -jnp.inf, sel3).max(axis=1, keepdims=True)
    gs = m1 + m2

    gid = lax.broadcasted_iota(I32, (N_GROUPS, 1, tm), 0)
    chosen = jnp.zeros((N_GROUPS, 1, tm), F32)
    for _ in range(TOPK_GROUPS):
        gm = gs.max(axis=0, keepdims=True)
        gfirst = jnp.where(gs == gm, gid, N_GROUPS).min(axis=0, keepdims=True)
        hit = gid == gfirst
        chosen = jnp.where(hit, 1.0, chosen)
        gs = jnp.where(hit, -jnp.inf, gs)

    eid = lax.broadcasted_iota(I32, (N_GROUPS, per, tm), 0) * per + member
    masked = jnp.where(chosen > 0.0, sel3, -jnp.inf)
    ids, raw = [], []
    for _ in range(TOP_K):
        mx = masked.max(axis=1, keepdims=True).max(axis=0, keepdims=True)
        efirst = jnp.where(masked == mx, eid, N_EXPERTS).min(axis=1, keepdims=True).min(axis=0, keepdims=True)
        hit = eid == efirst
        ids.append(efirst.reshape(1, tm))
        raw.append(jnp.where(hit, sc3, 0.0).sum(axis=1, keepdims=True).sum(axis=0, keepdims=True).reshape(1, tm))
        masked = jnp.where(hit, -jnp.inf, masked)
    ids = jnp.concatenate(ids, axis=0)
    raw = jnp.concatenate(raw, axis=0)
    gates = raw / raw.sum(axis=0, keepdims=True) * ROUTED_SCALE
    return ids, gates


def _post_attn_kernel(op_ref, os_ref, xp_ref, xs_ref, mod_ref, w_ref, pg_ref, fg_ref,
                      wrh_ref, wrl_ref, rb_ref,
                      y_ref, h_ref, ids_ref, gates_ref, *, ctx_tiles, tiles_per_row):
    i = pl.program_id(0)
    is_ctx = i < ctx_tiles
    row = jnp.where(is_ctx, 0, 1 + (i - ctx_tiles) // tiles_per_row)
    o = jnp.where(is_ctx, op_ref[...], os_ref[...])
    x = jnp.where(is_ctx, xp_ref[...], xs_ref[...])
    a = jnp.dot(o, w_ref[...], preferred_element_type=F32)
    y = x + _mod_chunk(mod_ref, row, 2) * _rms(a, pg_ref[...])
    y_ref[...] = y
    h = _rms(y, fg_ref[...]) * (1.0 + _mod_chunk(mod_ref, row, 4)) + _mod_chunk(mod_ref, row, 3)
    _store_rows(h_ref, h.shape[0], _pack_bf16_pair(h[:, :HALF], h[:, HALF:]))
    h_hi = h.astype(BF16)
    h_lo = (h - h_hi.astype(F32)).astype(BF16)
    wh, wl = wrh_ref[...], wrl_ref[...]
    logits = _dot_nt(wh, h_hi) + (_dot_nt(wl, h_hi) + _dot_nt(wh, h_lo))
    scores = 1.0 / (1.0 + jnp.exp(-logits))
    ids, gates = _route(scores + rb_ref[...], scores, logits.shape[1])
    ids_ref[...] = ids
    gates_ref[...] = gates


def _post_attn(o_p, o_s, x_p, x_s, mod, w_out_b, post_g, ffn_g, wr_hi, wr_lo, rbias, *, seq):
    tp, ts = x_p.shape[0], x_s.shape[0]
    t = tp + ts
    ctx_tiles = tp // TM
    full = lambda shape: pl.BlockSpec(shape, lambda i: (0,) * len(shape))
    ctx = lambda w: pl.BlockSpec((TM, w), lambda i: (jnp.minimum(i, ctx_tiles - 1), 0))
    lat = lambda w: pl.BlockSpec((TM, w), lambda i: (jnp.maximum(i - ctx_tiles, 0), 0))
    tok = lambda w: pl.BlockSpec((TM, w), lambda i: (i, 0))
    lane = pl.BlockSpec((TOP_K, TM), lambda i: (0, i))
    return pl.pallas_call(
        functools.partial(_post_attn_kernel, ctx_tiles=ctx_tiles, tiles_per_row=seq // TM),
        grid=(t // TM,),
        in_specs=[ctx(D_MODEL), lat(D_MODEL), ctx(D_MODEL), lat(D_MODEL), full(mod.shape),
                  full((D_MODEL, D_MODEL)), full((1, D_MODEL)), full((1, D_MODEL)),
                  full((N_EXPERTS, D_MODEL)), full((N_EXPERTS, D_MODEL)), full((N_EXPERTS, 1))],
        out_specs=[tok(D_MODEL), pl.BlockSpec((TM * CHUNKS, LANES), lambda i: (i, 0)), lane, lane],
        out_shape=[jax.ShapeDtypeStruct((t, D_MODEL), F32),
                   jax.ShapeDtypeStruct((t * CHUNKS, LANES), U32),
                   jax.ShapeDtypeStruct((TOP_K, t), I32),
                   jax.ShapeDtypeStruct((TOP_K, t), F32)],
        compiler_params=_cparams(("arbitrary",)),
        name="post_attn_router",
    )(o_p, o_s, x_p, x_s, mod, w_out_b, post_g, ffn_g, wr_hi, wr_lo, rbias)


def _stage_indices(idx_vmem, idx_smem, sem):
    cp = pltpu.make_async_copy(idx_vmem.at[0, 0], idx_smem, sem)
    cp.start()
    cp.wait()


def _invperm_kernel(dest_ref, order_ref, idx_smem, sem):
    _stage_indices(dest_ref, idx_smem, sem)
    base = pl.program_id(0) * (TM * TOP_K)

    def slot_body(j, carry):
        def chunk(c, carry):
            for u in range(ISSUE_UNROLL):
                tok = c * ISSUE_UNROLL + u
                order_ref[idx_smem[j * TM + tok]] = base + tok * TOP_K + j
            return carry
        return lax.fori_loop(0, TM // ISSUE_UNROLL, chunk, carry)
    lax.fori_loop(0, TOP_K, slot_body, 0)


def _invperm(dest_tiles):
    n_tiles = dest_tiles.shape[0]
    return pl.pallas_call(
        _invperm_kernel,
        grid=(n_tiles,),
        in_specs=[pl.BlockSpec((1, 1, TOP_K * TM), lambda i: (i, 0, 0))],
        out_specs=pl.BlockSpec(memory_space=pltpu.SMEM),
        out_shape=jax.ShapeDtypeStruct((n_tiles * TM * TOP_K,), I32),
        scratch_shapes=[pltpu.SMEM((TOP_K * TM,), I32), pltpu.SemaphoreType.DMA],
        compiler_params=_cparams(("arbitrary",)),
        name="invperm",
    )(dest_tiles)


def _experts_kernel(vb_ref, ve_ref, vlo_ref, vhi_ref, vfirst_ref, vlast_ref, nvis_ref,
                    ord_ref, h_hbm, wg_ref, wu_ref, wd_ref, yb_hbm,
                    idx_smem, xbuf, obuf, isem, gsem, ssem, wgb, wub, wdb, *, n_blocks):
    v = pl.program_id(0)
    nvis = nvis_ref[0]
    b = vb_ref[v]
    slot = b % 2

    def row(r):
        return pl.ds(pl.multiple_of(r * CHUNKS, CHUNKS), CHUNKS)

    def issue(first, copy_of):
        def chunk(c, carry):
            for u in range(ISSUE_UNROLL):
                r = c * ISSUE_UNROLL + u
                copy_of(r, idx_smem[first + r]).start()
            return carry
        lax.fori_loop(0, TE // ISSUE_UNROLL, chunk, 0)

    def gather(first, s):
        issue(first, lambda r, a: pltpu.make_async_copy(
            h_hbm.at[pl.ds(pl.multiple_of((a >> TOP_K_BITS) * CHUNKS, CHUNKS), CHUNKS)],
            xbuf.at[s, row(r)], gsem.at[s]))

    def scatter(s):
        issue(0, lambda r, a: pltpu.make_async_copy(
            obuf.at[s, row(r)],
            yb_hbm.at[a & (TOP_K - 1), pl.ds(pl.multiple_of((a >> TOP_K_BITS) * CHUNKS, CHUNKS), CHUNKS)], ssem.at[s]))

    def scatter_wait(s):
        pltpu.make_async_copy(obuf.at[s], yb_hbm.at[0, pl.ds(0, TE * CHUNKS)], ssem.at[s]).wait()

    @pl.when(v < nvis)
    def _():
        @pl.when(vfirst_ref[v] == 1)
        def _():
            _stage_indices(ord_ref, idx_smem, isem)

            @pl.when(v == 0)
            def _():
                gather(0, 0)

            @pl.when(b + 1 < n_blocks)
            def _():
                gather(TE, 1 - slot)

            pltpu.make_async_copy(h_hbm.at[pl.ds(0, TE * CHUNKS)], xbuf.at[slot], gsem.at[slot]).wait()

            @pl.when(b >= 2)
            def _():
                scatter_wait(slot)
            obuf[slot] = jnp.zeros((TE * CHUNKS, LANES), U32)

        prev = ve_ref[jnp.maximum(v - 1, 0)]

        @pl.when((v == 0) | (ve_ref[v] != prev))
        def _():
            wgb[...] = wg_ref[0].astype(BF16)
            wub[...] = wu_ref[0].astype(BF16)
            wdb[...] = wd_ref[0].astype(BF16)

        y = _swiglu_packed(_load_rows(xbuf.at[slot], TE), wgb, wub, wdb)
        rows = lax.broadcasted_iota(I32, (TE, 1), 0)
        mine = (rows >= vlo_ref[v]) & (rows < vhi_ref[v])
        y = jnp.where(mine, _pack_bf16_pair(y[:, :HALF], y[:, HALF:]), _load_rows(obuf.at[slot], TE))
        _store_rows(obuf.at[slot], TE, y)

        @pl.when(vlast_ref[v] == 1)
        def _():
            scatter(slot)

            @pl.when(v == nvis - 1)
            def _():
                scatter_wait(slot)
                scatter_wait(1 - slot)


def _experts(visits, order_pairs, h_packed, w_gate_e, w_up_e, w_down_e):
    n_visits = visits[0].shape[0]
    n_blocks = order_pairs.shape[0]
    n_tokens = h_packed.shape[0] // CHUNKS
    wspec = lambda shape: pl.BlockSpec((1,) + shape, lambda v, vb, ve, *_: (ve[v], 0, 0))
    grid_spec = pltpu.PrefetchScalarGridSpec(
        num_scalar_prefetch=7,
        grid=(n_visits,),
        in_specs=[pl.BlockSpec((1, 1, 2 * TE), lambda v, vb, *_: (vb[v], 0, 0)),
                  pl.BlockSpec(memory_space=pl.ANY),
                  wspec((D_MODEL, D_EXPERT)), wspec((D_MODEL, D_EXPERT)), wspec((D_EXPERT, D_MODEL))],
        out_specs=pl.BlockSpec(memory_space=pl.ANY),
        scratch_shapes=[pltpu.SMEM((2 * TE,), I32),
                        pltpu.VMEM((2, TE * CHUNKS, LANES), U32),
                        pltpu.VMEM((2, TE * CHUNKS, LANES), U32),
                        pltpu.SemaphoreType.DMA,
                        pltpu.SemaphoreType.DMA((2,)),
                        pltpu.SemaphoreType.DMA((2,)),
                        pltpu.VMEM((D_MODEL, D_EXPERT), BF16),
                        pltpu.VMEM((D_MODEL, D_EXPERT), BF16),
                        pltpu.VMEM((D_EXPERT, D_MODEL), BF16)])
    return pl.pallas_call(
        functools.partial(_experts_kernel, n_blocks=n_blocks),
        grid_spec=grid_spec,
        out_shape=jax.ShapeDtypeStruct((TOP_K, n_tokens * CHUNKS, LANES), U32),
        compiler_params=_cparams(("arbitrary",)),
        name="experts",
    )(*visits, order_pairs, h_packed, w_gate_e, w_up_e, w_down_e)


def _ffn_out_kernel(yb_ref, gates_ref, h_ref, y1_ref, mod_ref, fg_ref,
                    wg_ref, wu_ref, wd_ref, out_ref, *, tiles_per_row, mod_row0):
    i = pl.program_id(0)
    shared = _swiglu_packed(_load_rows(h_ref, TM), wg_ref, wu_ref, wd_ref)
    g = gates_ref[...]
    acc_lo = shared[:, :HALF]
    acc_hi = shared[:, HALF:]
    for j in range(TOP_K):
        r_lo, r_hi = _unpack_bf16_pair(_load_rows(yb_ref.at[j], TM))
        acc_lo = acc_lo + g[:, j:j + 1] * r_lo
        acc_hi = acc_hi + g[:, j:j + 1] * r_hi
    ffn = jnp.concatenate([acc_lo, acc_hi], axis=-1)
    row = mod_row0 + (i // tiles_per_row if tiles_per_row else 0)
    out_ref[...] = y1_ref[...] + _mod_chunk(mod_ref, row, 5) * _rms(ffn, fg_ref[...])


def _ffn_out(yb, gates_t, h_packed, y1, mod, post_g, wg_b, wu_b, wd_b,
             *, tile0, n_tiles, tiles_per_row, mod_row0):
    full = lambda shape: pl.BlockSpec(shape, lambda i: (0,) * len(shape))
    tok = lambda w: pl.BlockSpec((TM, w), lambda i: (tile0 + i, 0))
    return pl.pallas_call(
        functools.partial(_ffn_out_kernel, tiles_per_row=tiles_per_row, mod_row0=mod_row0),
        grid=(n_tiles,),
        in_specs=[pl.BlockSpec((TOP_K, TM * CHUNKS, LANES), lambda i: (0, tile0 + i, 0)),
                  tok(TOP_K), pl.BlockSpec((TM * CHUNKS, LANES), lambda i: (tile0 + i, 0)),
                  tok(D_MODEL), full(mod.shape), full((1, D_MODEL)),
                  full((D_MODEL, D_EXPERT)), full((D_MODEL, D_EXPERT)), full((D_EXPERT, D_MODEL))],
        out_specs=pl.BlockSpec((TM, D_MODEL), lambda i: (i, 0)),
        out_shape=jax.ShapeDtypeStruct((n_tiles * TM, D_MODEL), F32),
        compiler_params=_cparams(("arbitrary",)),
        name="ffn_out",
    )(yb, gates_t, h_packed, y1, mod, post_g, wg_b, wu_b, wd_b)


def _routing_tables(ids):
    t = ids.shape[1]
    n_blocks = t * TOP_K // TE
    n_visits = n_blocks + N_EXPERTS - 1
    ids_t = ids.T
    onehot = (ids_t[:, :, None] == jnp.arange(N_EXPERTS, dtype=I32)).sum(axis=1).astype(I32)
    before = jnp.cumsum(onehot, axis=0) - onehot
    counts = onehot.sum(axis=0)
    end = jnp.cumsum(counts)
    start = end - counts
    dest = start[ids_t] + jnp.take_along_axis(before, ids_t, axis=1)

    first_blk = start // TE
    n_vis_e = jnp.where(counts > 0, (end - 1) // TE - first_blk + 1, 0)
    vend = jnp.cumsum(n_vis_e)
    vstart = vend - n_vis_e
    n_vis = vend[-1]
    v = jnp.arange(n_visits, dtype=I32)
    valid = v < n_vis
    ve = jnp.minimum((vend[None, :] <= v[:, None]).sum(axis=1), N_EXPERTS - 1).astype(I32)
    ve = jnp.where(valid, ve, ve[n_vis - 1])
    vb = jnp.where(valid, first_blk[ve] + v - vstart[ve], n_blocks - 1).astype(I32)
    lo = jnp.where(valid, jnp.clip(start[ve] - vb * TE, 0, TE), 0).astype(I32)
    hi = jnp.where(valid, jnp.clip(end[ve] - vb * TE, 0, TE), 0).astype(I32)
    vfirst = (valid & ((v == 0) | (vb != jnp.roll(vb, 1)))).astype(I32)
    vlast = (valid & ((v == n_vis - 1) | (vb != jnp.roll(vb, -1)))).astype(I32)
    return dest.astype(I32), (vb, ve, lo, hi, vfirst, vlast, n_vis.reshape(1).astype(I32))


def _rope_tables(n_tokens):
    rows = n_tokens // GRID_W
    row = jnp.repeat(jnp.arange(rows, dtype=F32), GRID_W)
    col = jnp.tile(jnp.arange(GRID_W, dtype=F32), rows)
    inv_freq = ROPE_THETA ** (-jnp.arange(0, AXIS_DIM, 2, dtype=F32) / AXIS_DIM)
    ang_r = row[:, None] * inv_freq
    ang_c = col[:, None] * inv_freq
    ang = jnp.concatenate([ang_r, ang_r, ang_c, ang_c], axis=-1)
    cos, sin = jnp.cos(ang), jnp.sin(ang)
    first = (jnp.arange(HEAD_DIM) % AXIS_DIM) < (AXIS_DIM // 2)
    return cos, jnp.where(first, -sin, 0.0), jnp.where(first, 0.0, sin)


def kernel(x_prompt, x_sample, cache_glob_k, cache_glob_v, cache_win_k, cache_win_v, c, c_ctx,
           w_ada, b_ada, attn_pre_g, attn_post_g, w_in, q_norm_g, k_norm_g, sink_logit, w_out,
           ffn_pre_g, ffn_post_g, w_router, router_bias, w_gate_e, w_up_e, w_down_e,
           w_gate_s, w_up_s, w_down_s):
    batch, seq, d = x_prompt.shape
    dec_batch, dec_seq, _ = x_sample.shape
    past = cache_glob_k.shape[2]
    tp, ts = batch * seq, dec_batch * dec_seq
    t = tp + ts
    l = 0

    xp = x_prompt.reshape(tp, d)
    xs = x_sample.reshape(ts, d)
    cond = jnp.concatenate([c_ctx[None, :], c, jnp.zeros((MOD_ROWS - 1 - dec_batch, d), F32)], axis=0)
    mod = _adaln(cond, w_ada[l], b_ada[l][None, :])

    w_in_b = w_in[l].astype(BF16)
    pre_g = attn_pre_g[l][None, :]
    qn, kn = q_norm_g[l][None, :], k_norm_g[l][None, :]
    cos, sa, sb = _rope_tables(dec_seq)
    sink = sink_logit[l]

    qg, kg, vg, qw, kw, vw = _qkv(xp, mod, pre_g, w_in_b, qn, kn, cos, sa, sb,
                                  rope=False, seq=seq, kv_dtype=F32)
    o_p = _ctx_attention(sink, qg, kg, vg, qw, kw, vw, seq=seq)
    new_kv = [a.reshape(batch, 1, seq, N_KV, HEAD_DIM) for a in (kg, vg, kw, vw)]

    qg, kg, vg, qw, kw, vw = _qkv(xs, mod, pre_g, w_in_b, qn, kn, cos, sa, sb,
                                  rope=True, seq=dec_seq, kv_dtype=BF16)
    caches = [a[:, l].reshape(dec_batch * past, KV_W)
              for a in (cache_glob_k, cache_glob_v, cache_win_k, cache_win_v)]
    o_s = _lat_attention(sink, qg, kg, vg, qw, kw, vw, *caches, seq=dec_seq, past=past)

    wr_t = w_router[l].T
    wr_hi = wr_t.astype(BF16)
    wr_lo = (wr_t - wr_hi.astype(F32)).astype(BF16)
    y1, h_packed, ids, gates = _post_attn(
        o_p, o_s, xp, xs, mod, w_out[l].astype(BF16), attn_post_g[l][None, :], ffn_pre_g[l][None, :],
        wr_hi, wr_lo, router_bias[l][:, None], seq=dec_seq)

    dest, visits = _routing_tables(ids)
    dest_tiles = dest.reshape(t // TM, TM, TOP_K).transpose(0, 2, 1).reshape(t // TM, 1, TOP_K * TM)
    order = _invperm(dest_tiles).reshape(-1, TE)
    order_pairs = jnp.concatenate([order, jnp.concatenate([order[1:], order[-1:]], axis=0)], axis=1)
    yb = _experts(visits, order_pairs[:, None, :], h_packed, w_gate_e[l], w_up_e[l], w_down_e[l])

    gates_t = gates.T
    shared = (w_gate_s[l].astype(BF16), w_up_s[l].astype(BF16), w_down_s[l].astype(BF16))
    post_g = ffn_post_g[l][None, :]
    y_p = _ffn_out(yb, gates_t, h_packed, y1, mod, post_g, *shared,
                   tile0=0, n_tiles=tp // TM, tiles_per_row=0, mod_row0=0)
    y_s = _ffn_out(yb, gates_t, h_packed, y1, mod, post_g, *shared,
                   tile0=tp // TM, n_tiles=ts // TM, tiles_per_row=dec_seq // TM, mod_row0=1)
    return (y_p.reshape(batch, seq, d), y_s.reshape(dec_batch, dec_seq, d), *new_kv)
```

```python
import functools

import jax
import jax.numpy as jnp
from jax import lax
from jax.experimental import pallas as pl
from jax.experimental.pallas import tpu as pltpu

F32 = jnp.float32
BF16 = jnp.bfloat16
U32 = jnp.uint32
I32 = jnp.int32

D_MODEL = 2048
GRID_W = 64
HEAD_DIM = 128
AXIS_DIM = HEAD_DIM // 2
N_HEADS = 8
N_KV = 2
GROUP = N_HEADS // N_KV
Q_W = N_HEADS * HEAD_DIM
KV_W = N_KV * HEAD_DIM
QKV_WIDTH = 2 * Q_W + 4 * KV_W
WINDOW = 128
ROPE_THETA = 10000.0
NORM_EPS = 1e-6
N_EXPERTS = 64
N_GROUPS = 8
TOPK_GROUPS = 4
TOP_K = 8
D_EXPERT = 512
ROUTED_SCALE = 2.5

HALF = D_MODEL // 2
LANES = 128
CHUNKS = HALF // LANES
MOD_ROWS = 8
VMEM_LIMIT = 56 * 1024 * 1024

TM = 256
TQ = 256
TE = 256
NEG = -1e30
ISSUE_UNROLL = 32
DMA_THREADS = 2


def _cparams(sem):
    return pltpu.CompilerParams(dimension_semantics=sem, vmem_limit_bytes=VMEM_LIMIT)


def _pack_bf16_pair(lo, hi):
    lo_b = lax.bitcast_convert_type(lo.astype(BF16).astype(F32), U32)
    hi_b = lax.bitcast_convert_type(hi.astype(BF16).astype(F32), U32)
    return (hi_b & jnp.uint32(0xFFFF0000)) | (lo_b >> 16)


def _unpack_bf16_pair(p):
    lo = lax.bitcast_convert_type(p << 16, F32)
    hi = lax.bitcast_convert_type(p & jnp.uint32(0xFFFF0000), F32)
    return lo, hi


def _store_rows(ref, n, packed):
    for c in range(CHUNKS):
        ref[pl.ds(c, n, stride=CHUNKS), :] = packed[:, c * LANES:(c + 1) * LANES]


def _load_rows(ref, n, first=0):
    return jnp.concatenate(
        [ref[pl.ds(first * CHUNKS + c, n, stride=CHUNKS), :] for c in range(CHUNKS)], axis=-1)


def _rms(x, gain):
    r = lax.rsqrt(jnp.mean(x * x, axis=-1, keepdims=True) + NORM_EPS)
    return x * r * gain


def _dot_nt(a, b):
    return lax.dot_general(a, b, (((1,), (1,)), ((), ())), preferred_element_type=F32)


def _swiglu_packed(p, wg, wu, wd):
    lo, hi = _unpack_bf16_pair(p)
    lo, hi = lo.astype(BF16), hi.astype(BF16)
    gate = (jnp.dot(lo, wg[:HALF], preferred_element_type=F32)
            + jnp.dot(hi, wg[HALF:], preferred_element_type=F32))
    up = (jnp.dot(lo, wu[:HALF], preferred_element_type=F32)
          + jnp.dot(hi, wu[HALF:], preferred_element_type=F32))
    act = (gate / (1.0 + jnp.exp(-gate)) * up).astype(BF16)
    return jnp.dot(act, wd[...], preferred_element_type=F32)


def _adaln_kernel(cond_ref, w_ref, b_ref, o_ref):
    c = cond_ref[...]
    s = (c / (1.0 + jnp.exp(-c))).astype(BF16)
    o_ref[...] = jnp.dot(s, w_ref[...].astype(BF16), preferred_element_type=F32) + b_ref[...]


def _adaln(cond, w_ada, b_ada):
    n = w_ada.shape[1]
    tn = 1024
    return pl.pallas_call(
        _adaln_kernel,
        grid=(n // tn,),
        in_specs=[pl.BlockSpec((MOD_ROWS, D_MODEL), lambda j: (0, 0)),
                  pl.BlockSpec((D_MODEL, tn), lambda j: (0, j)),
                  pl.BlockSpec((1, tn), lambda j: (0, j))],
        out_specs=pl.BlockSpec((MOD_ROWS, tn), lambda j: (0, j)),
        out_shape=jax.ShapeDtypeStruct((MOD_ROWS, n), F32),
        compiler_params=_cparams(("arbitrary",)),
        name="adaln",
    )(cond, w_ada, b_ada)


def _mod_chunk(mod_ref, row, k):
    return mod_ref[pl.ds(row, 1), k * D_MODEL:(k + 1) * D_MODEL]


def _qkv_kernel(x_ref, mod_ref, g_ref, w_ref, qn_ref, kn_ref, cos_ref, sa_ref, sb_ref,
                qg_ref, kg_ref, vg_ref, qw_ref, kw_ref, vw_ref, *, rope, tiles_per_row):
    i = pl.program_id(0)
    row = (1 + i // tiles_per_row) if rope else 0
    x = x_ref[...]
    h = _rms(x, g_ref[...]) * (1.0 + _mod_chunk(mod_ref, row, 1)) + _mod_chunk(mod_ref, row, 0)
    proj = jnp.dot(h.astype(BF16), w_ref[...], preferred_element_type=F32)

    if rope:
        cos, sa, sb = cos_ref[...], sa_ref[...], sb_ref[...]

    def rot(t):
        if not rope:
            return t
        return t * cos + pltpu.roll(t, 96, 1) * sa + pltpu.roll(t, 32, 1) * sb

    scale = HEAD_DIM ** -0.5
    qn, kn = qn_ref[...], kn_ref[...]
    off = 0
    for hd in range(N_HEADS):
        t = proj[:, off + hd * HEAD_DIM: off + (hd + 1) * HEAD_DIM]
        qg_ref[:, hd * HEAD_DIM:(hd + 1) * HEAD_DIM] = (rot(_rms(t, qn)) * scale).astype(qg_ref.dtype)
    off += Q_W
    for hd in range(N_KV):
        t = proj[:, off + hd * HEAD_DIM: off + (hd + 1) * HEAD_DIM]
        kg_ref[:, hd * HEAD_DIM:(hd + 1) * HEAD_DIM] = rot(_rms(t, kn)).astype(kg_ref.dtype)
    off += KV_W
    vg_ref[...] = proj[:, off:off + KV_W].astype(vg_ref.dtype)
    off += KV_W
    for hd in range(N_HEADS):
        t = proj[:, off + hd * HEAD_DIM: off + (hd + 1) * HEAD_DIM]
        qw_ref[:, hd * HEAD_DIM:(hd + 1) * HEAD_DIM] = (rot(t) * scale).astype(qw_ref.dtype)
    off += Q_W
    for hd in range(N_KV):
        t = proj[:, off + hd * HEAD_DIM: off + (hd + 1) * HEAD_DIM]
        kw_ref[:, hd * HEAD_DIM:(hd + 1) * HEAD_DIM] = rot(t).astype(kw_ref.dtype)
    off += KV_W
    vw_ref[...] = proj[:, off:off + KV_W].astype(vw_ref.dtype)


def _qkv(x2d, mod, pre_g, w_in_b, qn, kn, cos, sa, sb, *, rope, seq, kv_dtype):
    t = x2d.shape[0]
    tiles_per_row = seq // TM
    full = lambda shape: pl.BlockSpec(shape, lambda i: (0,) * len(shape))
    tab = pl.BlockSpec((TM, HEAD_DIM), (lambda i: (i % tiles_per_row, 0)) if rope else (lambda i: (0, 0)))
    tok = lambda w: pl.BlockSpec((TM, w), lambda i: (i, 0))
    return pl.pallas_call(
        functools.partial(_qkv_kernel, rope=rope, tiles_per_row=tiles_per_row),
        grid=(t // TM,),
        in_specs=[tok(D_MODEL), full(mod.shape), full((1, D_MODEL)),
                  full((D_MODEL, QKV_WIDTH)), full((1, HEAD_DIM)), full((1, HEAD_DIM)),
                  tab, tab, tab],
        out_specs=[tok(Q_W), tok(KV_W), tok(KV_W), tok(Q_W), tok(KV_W), tok(KV_W)],
        out_shape=[jax.ShapeDtypeStruct((t, Q_W), BF16),
                   jax.ShapeDtypeStruct((t, KV_W), kv_dtype),
                   jax.ShapeDtypeStruct((t, KV_W), kv_dtype),
                   jax.ShapeDtypeStruct((t, Q_W), BF16),
                   jax.ShapeDtypeStruct((t, KV_W), kv_dtype),
                   jax.ShapeDtypeStruct((t, KV_W), kv_dtype)],
        compiler_params=_cparams(("arbitrary",)),
        name="qkv_rope" if rope else "qkv_ctx",
    )(x2d, mod, pre_g, w_in_b, qn, kn, cos, sa, sb)


def _softmax_pv(scores, values, sink):
    m = scores[0].max(axis=-1, keepdims=True)
    for s in scores[1:]:
        m = jnp.maximum(m, s.max(axis=-1, keepdims=True))
    if sink is not None:
        m = jnp.maximum(m, sink)
    den = jnp.exp(sink - m) if sink is not None else 0.0
    acc = None
    for s, v in zip(scores, values):
        p = jnp.exp(s - m)
        den = den + p.sum(axis=-1, keepdims=True)
        pv = jnp.dot(p.astype(BF16), v, preferred_element_type=F32)
        acc = pv if acc is None else acc + pv
    return acc / den


def _ctx_attn_kernel(sink_ref, qg_ref, kg_ref, vg_ref, qw_ref, kw_ref, vw_ref, o_ref):
    for mixer, (q_ref, k_ref, v_ref) in enumerate(((qg_ref, kg_ref, vg_ref), (qw_ref, kw_ref, vw_ref))):
        for n in range(N_KV):
            k = k_ref[:, n * HEAD_DIM:(n + 1) * HEAD_DIM].astype(BF16)
            v = v_ref[:, n * HEAD_DIM:(n + 1) * HEAD_DIM].astype(BF16)
            for g in range(GROUP):
                hd = n * GROUP + g
                q = q_ref[:, hd * HEAD_DIM:(hd + 1) * HEAD_DIM]
                sink = sink_ref[hd] if mixer == 1 else None
                o = _softmax_pv([_dot_nt(q, k)], [v], sink)
                c0 = mixer * Q_W + hd * HEAD_DIM
                o_ref[:, c0:c0 + HEAD_DIM] = o.astype(o_ref.dtype)


def _ctx_attention(sink, qg, kg, vg, qw, kw, vw, *, seq):
    t = qg.shape[0]
    blk = lambda w: pl.BlockSpec((seq, w), lambda b: (b, 0))
    return pl.pallas_call(
        _ctx_attn_kernel,
        grid=(t // seq,),
        in_specs=[pl.BlockSpec(memory_space=pltpu.SMEM),
                  blk(Q_W), blk(KV_W), blk(KV_W), blk(Q_W), blk(KV_W), blk(KV_W)],
        out_specs=blk(2 * Q_W),
        out_shape=jax.ShapeDtypeStruct((t, 2 * Q_W), BF16),
        compiler_params=_cparams(("arbitrary",)),
        name="attn_ctx",
    )(sink, qg, kg, vg, qw, kw, vw)


def _lat_attn_kernel(sink_ref, qg_ref, kg_ref, vg_ref, qw_ref, kw_ref, vw_ref,
                     cgk_ref, cgv_ref, cwk_ref, cwv_ref, o_ref, *, seq):
    qi = pl.program_id(1)
    span = TQ + 2 * WINDOW
    start = pl.multiple_of(jnp.clip(qi * TQ - WINDOW, 0, seq - span), WINDOW)
    qpos = qi * TQ + lax.broadcasted_iota(I32, (TQ, span), 0)
    kpos = start + lax.broadcasted_iota(I32, (TQ, span), 1)
    band_ok = jnp.abs(kpos - qpos) <= WINDOW
    for n in range(N_KV):
        cols = slice(n * HEAD_DIM, (n + 1) * HEAD_DIM)
        kc, vc = cgk_ref[:, cols].astype(BF16), cgv_ref[:, cols].astype(BF16)
        kl, vl = kg_ref[:, cols], vg_ref[:, cols]
        for g in range(GROUP):
            hd = n * GROUP + g
            q = qg_ref[:, hd * HEAD_DIM:(hd + 1) * HEAD_DIM]
            o = _softmax_pv([_dot_nt(q, kc), _dot_nt(q, kl)], [vc, vl], None)
            o_ref[:, hd * HEAD_DIM:(hd + 1) * HEAD_DIM] = o.astype(o_ref.dtype)
        kc, vc = cwk_ref[:, cols].astype(BF16), cwv_ref[:, cols].astype(BF16)
        kl, vl = kw_ref[pl.ds(start, span), cols], vw_ref[pl.ds(start, span), cols]
        for g in range(GROUP):
            hd = n * GROUP + g
            q = qw_ref[:, hd * HEAD_DIM:(hd + 1) * HEAD_DIM]
            band = jnp.where(band_ok, _dot_nt(q, kl), NEG)
            o = _softmax_pv([_dot_nt(q, kc), band], [vc, vl], sink_ref[hd])
            c0 = Q_W + hd * HEAD_DIM
            o_ref[:, c0:c0 + HEAD_DIM] = o.astype(o_ref.dtype)


def _lat_attention(sink, qg, kg, vg, qw, kw, vw, cgk, cgv, cwk, cwv, *, seq, past):
    t = qg.shape[0]
    nq = seq // TQ
    qblk = pl.BlockSpec((TQ, Q_W), lambda b, i: (b * nq + i, 0))
    kblk = pl.BlockSpec((seq, KV_W), lambda b, i: (b, 0))
    cblk = pl.BlockSpec((past, KV_W), lambda b, i: (b, 0))
    return pl.pallas_call(
        functools.partial(_lat_attn_kernel, seq=seq),
        grid=(t // seq, nq),
        in_specs=[pl.BlockSpec(memory_space=pltpu.SMEM),
                  qblk, kblk, kblk, qblk, kblk, kblk, cblk, cblk, cblk, cblk],
        out_specs=pl.BlockSpec((TQ, 2 * Q_W), lambda b, i: (b * nq + i, 0)),
        out_shape=jax.ShapeDtypeStruct((t, 2 * Q_W), BF16),
        compiler_params=_cparams(("arbitrary", "arbitrary")),
        name="attn_lat",
    )(sink, qg, kg, vg, qw, kw, vw, cgk, cgv, cwk, cwv)


def _route(sel, scores, tm):
    per = N_EXPERTS // N_GROUPS
    sel3 = sel.reshape(N_GROUPS, per, tm)
    sc3 = scores.reshape(N_GROUPS, per, tm)
    member = lax.broadcasted_iota(I32, (N_GROUPS, per, tm), 1)
    m1 = sel3.max(axis=1, keepdims=True)
    first = jnp.where(sel3 == m1, member, per).min(axis=1, keepdims=True)
    m2 = jnp.where(member == first, -jnp.inf, sel3).max(axis=1, keepdims=True)
    gs = m1 + m2

    gid = lax.broadcasted_iota(I32, (N_GROUPS, 1, tm), 0)
    chosen = jnp.zeros((N_GROUPS, 1, tm), F32)
    for _ in range(TOPK_GROUPS):
        gm = gs.max(axis=0, keepdims=True)
        gfirst = jnp.where(gs == gm, gid, N_GROUPS).min(axis=0, keepdims=True)
        hit = gid == gfirst
        chosen = jnp.where(hit, 1.0, chosen)
        gs = jnp.where(hit, -jnp.inf, gs)

    eid = lax.broadcasted_iota(I32, (N_GROUPS, per, tm), 0) * per + member
    masked = jnp.where(chosen > 0.0, sel3, -jnp.inf)
    ids, raw = [], []
    for _ in range(TOP_K):
        mx = masked.max(axis=1, keepdims=True).max(axis=0, keepdims=True)
        efirst = jnp.where(masked == mx, eid, N_EXPERTS).min(axis=1, keepdims=True).min(axis=0, keepdims=True)
        hit = eid == efirst
        ids.append(efirst.reshape(1, tm))
        raw.append(jnp.where(hit, sc3, 0.0).sum(axis=1, keepdims=True).sum(axis=0, keepdims=True).reshape(1, tm))
        masked = jnp.where(hit, -jnp.inf, masked)
    ids = jnp.concatenate(ids, axis=0)
    raw = jnp.concatenate(raw, axis=0)
    gates = raw / raw.sum(axis=0, keepdims=True) * ROUTED_SCALE
    return ids, gates


def _post_attn_kernel(op_ref, os_ref, xp_ref, xs_ref, mod_ref, w_ref, pg_ref, fg_ref,
                      wrh_ref, wrl_ref, rb_ref,
                      y_ref, h_ref, ids_ref, gates_ref, *, ctx_tiles, tiles_per_row):
    i = pl.program_id(0)
    is_ctx = i < ctx_tiles
    row = jnp.where(is_ctx, 0, 1 + (i - ctx_tiles) // tiles_per_row)
    o = jnp.where(is_ctx, op_ref[...], os_ref[...])
    x = jnp.where(is_ctx, xp_ref[...], xs_ref[...])
    a = jnp.dot(o, w_ref[...], preferred_element_type=F32)
    y = x + _mod_chunk(mod_ref, row, 2) * _rms(a, pg_ref[...])
    y_ref[...] = y
    h = _rms(y, fg_ref[...]) * (1.0 + _mod_chunk(mod_ref, row, 4)) + _mod_chunk(mod_ref, row, 3)
    _store_rows(h_ref, h.shape[0], _pack_bf16_pair(h[:, :HALF], h[:, HALF:]))
    h_hi = h.astype(BF16)
    h_lo = (h - h_hi.astype(F32)).astype(BF16)
    wh, wl = wrh_ref[...], wrl_ref[...]
    logits = _dot_nt(wh, h_hi) + (_dot_nt(wl, h_hi) + _dot_nt(wh, h_lo))
    scores = 1.0 / (1.0 + jnp.exp(-logits))
    ids, gates = _route(scores + rb_ref[...], scores, logits.shape[1])
    ids_ref[...] = ids
    gates_ref[...] = gates


def _post_attn(o_p, o_s, x_p, x_s, mod, w_out_b, post_g, ffn_g, wr_hi, wr_lo, rbias, *, seq):
    tp, ts = x_p.shape[0], x_s.shape[0]
    t = tp + ts
    ctx_tiles = tp // TM
    full = lambda shape: pl.BlockSpec(shape, lambda i: (0,) * len(shape))
    ctx = lambda w: pl.BlockSpec((TM, w), lambda i: (jnp.minimum(i, ctx_tiles - 1), 0))
    lat = lambda w: pl.BlockSpec((TM, w), lambda i: (jnp.maximum(i - ctx_tiles, 0), 0))
    tok = lambda w: pl.BlockSpec((TM, w), lambda i: (i, 0))
    lane = pl.BlockSpec((TOP_K, TM), lambda i: (0, i))
    return pl.pallas_call(
        functools.partial(_post_attn_kernel, ctx_tiles=ctx_tiles, tiles_per_row=seq // TM),
        grid=(t // TM,),
        in_specs=[ctx(D_MODEL), lat(D_MODEL), ctx(D_MODEL), lat(D_MODEL), full(mod.shape),
                  full((D_MODEL, D_MODEL)), full((1, D_MODEL)), full((1, D_MODEL)),
                  full((N_EXPERTS, D_MODEL)), full((N_EXPERTS, D_MODEL)), full((N_EXPERTS, 1))],
        out_specs=[tok(D_MODEL), pl.BlockSpec((TM * CHUNKS, LANES), lambda i: (i, 0)), lane, lane],
        out_shape=[jax.ShapeDtypeStruct((t, D_MODEL), F32),
                   jax.ShapeDtypeStruct((t * CHUNKS, LANES), U32),
                   jax.ShapeDtypeStruct((TOP_K, t), I32),
                   jax.ShapeDtypeStruct((TOP_K, t), F32)],
        compiler_params=_cparams(("arbitrary",)),
        name="post_attn_router",
    )(o_p, o_s, x_p, x_s, mod, w_out_b, post_g, ffn_g, wr_hi, wr_lo, rbias)


def _stage_indices(idx_vmem, idx_smem, sem):
    cp = pltpu.make_async_copy(idx_vmem.at[0, 0], idx_smem, sem)
    cp.start()
    cp.wait()


def _issue_row_dmas(idx_smem, copy_of):
    def slot_body(j, carry):
        def chunk(c, carry):
            for u in range(ISSUE_UNROLL):
                tok = c * ISSUE_UNROLL + u
                copy_of(j, tok, pl.multiple_of(idx_smem[j * TM + tok], CHUNKS)).start(
                    priority=u % DMA_THREADS)
            return carry
        return lax.fori_loop(0, TM // ISSUE_UNROLL, chunk, carry)
    lax.fori_loop(0, TOP_K, slot_body, 0)


def _dispatch_kernel(idx_ref, h_ref, xs_hbm, idx_smem, isem, dsem):
    _stage_indices(idx_ref, idx_smem, isem)

    def copy_of(j, tok, dst):
        src = pl.multiple_of(tok * CHUNKS, CHUNKS)
        return pltpu.make_async_copy(h_ref.at[pl.ds(src, CHUNKS)], xs_hbm.at[pl.ds(dst, CHUNKS)], dsem)

    _issue_row_dmas(idx_smem, copy_of)
    for _ in range(TOP_K):
        pltpu.make_async_copy(h_ref, xs_hbm.at[pl.ds(0, TM * CHUNKS)], dsem).wait()


def _dispatch(dest_tiles, h_packed):
    n_tiles = dest_tiles.shape[0]
    rows = n_tiles * TM * TOP_K
    return pl.pallas_call(
        _dispatch_kernel,
        grid=(n_tiles,),
        in_specs=[pl.BlockSpec((1, 1, TOP_K * TM), lambda i: (i, 0, 0)),
                  pl.BlockSpec((TM * CHUNKS, LANES), lambda i: (i, 0))],
        out_specs=pl.BlockSpec(memory_space=pl.ANY),
        out_shape=jax.ShapeDtypeStruct((rows * CHUNKS, LANES), U32),
        scratch_shapes=[pltpu.SMEM((TOP_K * TM,), I32),
                        pltpu.SemaphoreType.DMA,
                        pltpu.SemaphoreType.DMA],
        compiler_params=_cparams(("arbitrary",)),
        name="dispatch",
    )(dest_tiles, h_packed)


def _experts_kernel(vb_ref, ve_ref, vlo_ref, vhi_ref, vfirst_ref, nvis_ref,
                    x_ref, wg_ref, wu_ref, wd_ref, out_ref, wgb, wub, wdb):
    v = pl.program_id(0)

    @pl.when(v < nvis_ref[0])
    def _():
        prev = ve_ref[jnp.maximum(v - 1, 0)]

        @pl.when((v == 0) | (ve_ref[v] != prev))
        def _():
            wgb[...] = wg_ref[0].astype(BF16)
            wub[...] = wu_ref[0].astype(BF16)
            wdb[...] = wd_ref[0].astype(BF16)

        @pl.when(vfirst_ref[v] == 1)
        def _():
            out_ref[...] = jnp.zeros_like(out_ref)

        y = _swiglu_packed(_load_rows(x_ref, TE), wgb, wub, wdb)
        rows = lax.broadcasted_iota(I32, (TE, 1), 0)
        mine = (rows >= vlo_ref[v]) & (rows < vhi_ref[v])
        y = jnp.where(mine, _pack_bf16_pair(y[:, :HALF], y[:, HALF:]), _load_rows(out_ref, TE))
        _store_rows(out_ref, TE, y)


def _experts(visits, xs, w_gate_e, w_up_e, w_down_e):
    n_visits = visits[0].shape[0]
    n_rows = xs.shape[0] // CHUNKS
    wspec = lambda shape: pl.BlockSpec((1,) + shape, lambda v, vb, ve, *_: (ve[v], 0, 0))
    rows = pl.BlockSpec((TE * CHUNKS, LANES), lambda v, vb, *_: (vb[v], 0))
    grid_spec = pltpu.PrefetchScalarGridSpec(
        num_scalar_prefetch=6,
        grid=(n_visits,),
        in_specs=[rows, wspec((D_MODEL, D_EXPERT)), wspec((D_MODEL, D_EXPERT)), wspec((D_EXPERT, D_MODEL))],
        out_specs=rows,
        scratch_shapes=[pltpu.VMEM((D_MODEL, D_EXPERT), BF16),
                        pltpu.VMEM((D_MODEL, D_EXPERT), BF16),
                        pltpu.VMEM((D_EXPERT, D_MODEL), BF16)])
    return pl.pallas_call(
        _experts_kernel,
        grid_spec=grid_spec,
        out_shape=jax.ShapeDtypeStruct((n_rows * CHUNKS, LANES), U32),
        compiler_params=_cparams(("arbitrary",)),
        name="experts",
    )(*visits, xs, w_gate_e, w_up_e, w_down_e)


def _ffn_out_kernel(cur_ref, nxt_ref, yb_hbm, gates_ref, h_ref, y1_ref, mod_ref, fg_ref,
                    wg_ref, wu_ref, wd_ref, out_ref, idx_smem, gbuf, isem, gsem,
                    *, n_tiles, tiles_per_row, mod_row0):
    i = pl.program_id(0)
    slot = i % 2
    n_idx = TOP_K * TM

    def gather(idx_vmem, s):
        _stage_indices(idx_vmem, idx_smem, isem)
        _issue_row_dmas(idx_smem, lambda j, tok, src: pltpu.make_async_copy(
            yb_hbm.at[pl.ds(src, CHUNKS)],
            gbuf.at[s, pl.ds(pl.multiple_of((j * TM + tok) * CHUNKS, CHUNKS), CHUNKS)], gsem.at[s]))

    @pl.when(i == 0)
    def _():
        gather(cur_ref, 0)

    @pl.when(i + 1 < n_tiles)
    def _():
        gather(nxt_ref, 1 - slot)

    shared = _swiglu_packed(_load_rows(h_ref, TM), wg_ref, wu_ref, wd_ref)

    pltpu.make_async_copy(yb_hbm.at[pl.ds(0, n_idx * CHUNKS)], gbuf.at[slot], gsem.at[slot]).wait()
    g = gates_ref[...]
    acc_lo = shared[:, :HALF]
    acc_hi = shared[:, HALF:]
    for j in range(TOP_K):
        r_lo, r_hi = _unpack_bf16_pair(_load_rows(gbuf.at[slot], TM, first=j * TM))
        acc_lo = acc_lo + g[:, j:j + 1] * r_lo
        acc_hi = acc_hi + g[:, j:j + 1] * r_hi
    ffn = jnp.concatenate([acc_lo, acc_hi], axis=-1)
    row = mod_row0 + (i // tiles_per_row if tiles_per_row else 0)
    out_ref[...] = y1_ref[...] + _mod_chunk(mod_ref, row, 5) * _rms(ffn, fg_ref[...])


def _ffn_out(dest_tiles, yb, gates_t, h_packed, y1, mod, post_g, wg_b, wu_b, wd_b,
             *, tile0, n_tiles, tiles_per_row, mod_row0):
    full = lambda shape: pl.BlockSpec(shape, lambda i: (0,) * len(shape))
    tok = lambda w: pl.BlockSpec((TM, w), lambda i: (tile0 + i, 0))
    last = tile0 + n_tiles - 1
    idx = lambda step: pl.BlockSpec((1, 1, TOP_K * TM),
                                    lambda i: (jnp.minimum(tile0 + i + step, last), 0, 0))
    return pl.pallas_call(
        functools.partial(_ffn_out_kernel, n_tiles=n_tiles,
                          tiles_per_row=tiles_per_row, mod_row0=mod_row0),
        grid=(n_tiles,),
        in_specs=[idx(0), idx(1), pl.BlockSpec(memory_space=pl.ANY),
                  tok(TOP_K), pl.BlockSpec((TM * CHUNKS, LANES), lambda i: (tile0 + i, 0)),
                  tok(D_MODEL), full(mod.shape), full((1, D_MODEL)),
                  full((D_MODEL, D_EXPERT)), full((D_MODEL, D_EXPERT)), full((D_EXPERT, D_MODEL))],
        out_specs=pl.BlockSpec((TM, D_MODEL), lambda i: (i, 0)),
        out_shape=jax.ShapeDtypeStruct((n_tiles * TM, D_MODEL), F32),
        scratch_shapes=[pltpu.SMEM((TOP_K * TM,), I32),
                        pltpu.VMEM((2, TOP_K * TM * CHUNKS, LANES), U32),
                        pltpu.SemaphoreType.DMA,
                        pltpu.SemaphoreType.DMA((2,))],
        compiler_params=_cparams(("arbitrary",)),
        name="ffn_out",
    )(dest_tiles, dest_tiles, yb, gates_t, h_packed, y1, mod, post_g, wg_b, wu_b, wd_b)


def _routing_tables(ids):
    t = ids.shape[1]
    n_blocks = t * TOP_K // TE
    n_visits = n_blocks + N_EXPERTS - 1
    ids_t = ids.T
    onehot = (ids_t[:, :, None] == jnp.arange(N_EXPERTS, dtype=I32)).sum(axis=1).astype(I32)
    before = jnp.cumsum(onehot, axis=0) - onehot
    counts = onehot.sum(axis=0)
    end = jnp.cumsum(counts)
    start = end - counts
    dest = start[ids_t] + jnp.take_along_axis(before, ids_t, axis=1)

    first_blk = start // TE
    n_vis_e = jnp.where(counts > 0, (end - 1) // TE - first_blk + 1, 0)
    vend = jnp.cumsum(n_vis_e)
    vstart = vend - n_vis_e
    n_vis = vend[-1]
    v = jnp.arange(n_visits, dtype=I32)
    valid = v < n_vis
    ve = jnp.minimum((vend[None, :] <= v[:, None]).sum(axis=1), N_EXPERTS - 1).astype(I32)
    ve = jnp.where(valid, ve, ve[n_vis - 1])
    vb = jnp.where(valid, first_blk[ve] + v - vstart[ve], n_blocks - 1).astype(I32)
    lo = jnp.where(valid, jnp.clip(start[ve] - vb * TE, 0, TE), 0).astype(I32)
    hi = jnp.where(valid, jnp.clip(end[ve] - vb * TE, 0, TE), 0).astype(I32)
    vfirst = (valid & ((v == 0) | (vb != jnp.roll(vb, 1)))).astype(I32)
    return dest.astype(I32), (vb, ve, lo, hi, vfirst, n_vis.reshape(1).astype(I32))


def _rope_tables(n_tokens):
    rows = n_tokens // GRID_W
    row = jnp.repeat(jnp.arange(rows, dtype=F32), GRID_W)
    col = jnp.tile(jnp.arange(GRID_W, dtype=F32), rows)
    inv_freq = ROPE_THETA ** (-jnp.arange(0, AXIS_DIM, 2, dtype=F32) / AXIS_DIM)
    ang_r = row[:, None] * inv_freq
    ang_c = col[:, None] * inv_freq
    ang = jnp.concatenate([ang_r, ang_r, ang_c, ang_c], axis=-1)
    cos, sin = jnp.cos(ang), jnp.sin(ang)
    first = (jnp.arange(HEAD_DIM) % AXIS_DIM) < (AXIS_DIM // 2)
    return cos, jnp.where(first, -sin, 0.0), jnp.where(first, 0.0, sin)


def kernel(x_prompt, x_sample, cache_glob_k, cache_glob_v, cache_win_k, cache_win_v, c, c_ctx,
           w_ada, b_ada, attn_pre_g, attn_post_g, w_in, q_norm_g, k_norm_g, sink_logit, w_out,
           ffn_pre_g, ffn_post_g, w_router, router_bias, w_gate_e, w_up_e, w_down_e,
           w_gate_s, w_up_s, w_down_s):
    batch, seq, d = x_prompt.shape
    dec_batch, dec_seq, _ = x_sample.shape
    past = cache_glob_k.shape[2]
    tp, ts = batch * seq, dec_batch * dec_seq
    t = tp + ts
    l = 0

    xp = x_prompt.reshape(tp, d)
    xs = x_sample.reshape(ts, d)
    cond = jnp.concatenate([c_ctx[None, :], c, jnp.zeros((MOD_ROWS - 1 - dec_batch, d), F32)], axis=0)
    mod = _adaln(cond, w_ada[l], b_ada[l][None, :])

    w_in_b = w_in[l].astype(BF16)
    pre_g = attn_pre_g[l][None, :]
    qn, kn = q_norm_g[l][None, :], k_norm_g[l][None, :]
    cos, sa, sb = _rope_tables(dec_seq)
    sink = sink_logit[l]

    qg, kg, vg, qw, kw, vw = _qkv(xp, mod, pre_g, w_in_b, qn, kn, cos, sa, sb,
                                  rope=False, seq=seq, kv_dtype=F32)
    o_p = _ctx_attention(sink, qg, kg, vg, qw, kw, vw, seq=seq)
    new_kv = [a.reshape(batch, 1, seq, N_KV, HEAD_DIM) for a in (kg, vg, kw, vw)]

    qg, kg, vg, qw, kw, vw = _qkv(xs, mod, pre_g, w_in_b, qn, kn, cos, sa, sb,
                                  rope=True, seq=dec_seq, kv_dtype=BF16)
    caches = [a[:, l].reshape(dec_batch * past, KV_W)
              for a in (cache_glob_k, cache_glob_v, cache_win_k, cache_win_v)]
    o_s = _lat_attention(sink, qg, kg, vg, qw, kw, vw, *caches, seq=dec_seq, past=past)

    wr_t = w_router[l].T
    wr_hi = wr_t.astype(BF16)
    wr_lo = (wr_t - wr_hi.astype(F32)).astype(BF16)
    y1, h_packed, ids, gates = _post_attn(
        o_p, o_s, xp, xs, mod, w_out[l].astype(BF16), attn_post_g[l][None, :], ffn_pre_g[l][None, :],
        wr_hi, wr_lo, router_bias[l][:, None], seq=dec_seq)

    dest, visits = _routing_tables(ids)
    dest_tiles = (dest * CHUNKS).reshape(t // TM, TM, TOP_K).transpose(0, 2, 1).reshape(t // TM, 1, TOP_K * TM)
    xs_rows = _dispatch(dest_tiles, h_packed)
    yb = _experts(visits, xs_rows, w_gate_e[l], w_up_e[l], w_down_e[l])

    gates_t = gates.T
    shared = (w_gate_s[l].astype(BF16), w_up_s[l].astype(BF16), w_down_s[l].astype(BF16))
    post_g = ffn_post_g[l][None, :]
    y_p = _ffn_out(dest_tiles, yb, gates_t, h_packed, y1, mod, post_g, *shared,
                   tile0=0, n_tiles=tp // TM, tiles_per_row=0, mod_row0=0)
    y_s = _ffn_out(dest_tiles, yb, gates_t, h_packed, y1, mod, post_g, *shared,
                   tile0=tp // TM, n_tiles=ts // TM, tiles_per_row=dec_seq // TM, mod_row0=1)
    return (y_p.reshape(batch, seq, d), y_s.reshape(dec_batch, dec_seq, d), *new_kv)
```

```python
import functools

import jax
import jax.numpy as jnp
from jax import lax
from jax.experimental import pallas as pl
from jax.experimental.pallas import tpu as pltpu

F32 = jnp.float32
BF16 = jnp.bfloat16
U32 = jnp.uint32
I32 = jnp.int32

D_MODEL = 2048
GRID_W = 64
HEAD_DIM = 128
AXIS_DIM = HEAD_DIM // 2
N_HEADS = 8
N_KV = 2
GROUP = N_HEADS // N_KV
Q_W = N_HEADS * HEAD_DIM
KV_W = N_KV * HEAD_DIM
QKV_WIDTH = 2 * Q_W + 4 * KV_W
WINDOW = 128
ROPE_THETA = 10000.0
NORM_EPS = 1e-6
N_EXPERTS = 64
N_GROUPS = 8
TOPK_GROUPS = 4
TOP_K = 8
D_EXPERT = 512
ROUTED_SCALE = 2.5

HALF = D_MODEL // 2
LANES = 128
CHUNKS = HALF // LANES
MOD_ROWS = 8
VMEM_LIMIT = 56 * 1024 * 1024

TM = 256
TQ = 256
TE = 256
NEG = -1e30
ISSUE_UNROLL = 32
DMA_THREADS = 2


def _cparams(sem):
    return pltpu.CompilerParams(dimension_semantics=sem, vmem_limit_bytes=VMEM_LIMIT)


def _pack_bf16_pair(lo, hi):
    lo_b = lax.bitcast_convert_type(lo.astype(BF16).astype(F32), U32)
    hi_b = lax.bitcast_convert_type(hi.astype(BF16).astype(F32), U32)
    return (hi_b & jnp.uint32(0xFFFF0000)) | (lo_b >> 16)


def _unpack_bf16_pair(p):
    lo = lax.bitcast_convert_type(p << 16, F32)
    hi = lax.bitcast_convert_type(p & jnp.uint32(0xFFFF0000), F32)
    return lo, hi


def _store_rows(ref, n, packed):
    for c in range(CHUNKS):
        ref[pl.ds(c, n, stride=CHUNKS), :] = packed[:, c * LANES:(c + 1) * LANES]


def _load_rows(ref, n, first=0):
    return jnp.concatenate(
        [ref[pl.ds(first * CHUNKS + c, n, stride=CHUNKS), :] for c in range(CHUNKS)], axis=-1)


def _rms(x, gain):
    r = lax.rsqrt(jnp.mean(x * x, axis=-1, keepdims=True) + NORM_EPS)
    return x * r * gain


def _dot_nt(a, b):
    return lax.dot_general(a, b, (((1,), (1,)), ((), ())), preferred_element_type=F32)


def _swiglu_packed(p, wg, wu, wd):
    lo, hi = _unpack_bf16_pair(p)
    lo, hi = lo.astype(BF16), hi.astype(BF16)
    gate = (jnp.dot(lo, wg[:HALF], preferred_element_type=F32)
            + jnp.dot(hi, wg[HALF:], preferred_element_type=F32))
    up = (jnp.dot(lo, wu[:HALF], preferred_element_type=F32)
          + jnp.dot(hi, wu[HALF:], preferred_element_type=F32))
    act = (gate / (1.0 + jnp.exp(-gate)) * up).astype(BF16)
    return jnp.dot(act, wd[...], preferred_element_type=F32)


def _adaln_kernel(cond_ref, w_ref, b_ref, o_ref):
    c = cond_ref[...]
    s = (c / (1.0 + jnp.exp(-c))).astype(BF16)
    o_ref[...] = jnp.dot(s, w_ref[...].astype(BF16), preferred_element_type=F32) + b_ref[...]


def _adaln(cond, w_ada, b_ada):
    n = w_ada.shape[1]
    tn = 1024
    return pl.pallas_call(
        _adaln_kernel,
        grid=(n // tn,),
        in_specs=[pl.BlockSpec((MOD_ROWS, D_MODEL), lambda j: (0, 0)),
                  pl.BlockSpec((D_MODEL, tn), lambda j: (0, j)),
                  pl.BlockSpec((1, tn), lambda j: (0, j))],
        out_specs=pl.BlockSpec((MOD_ROWS, tn), lambda j: (0, j)),
        out_shape=jax.ShapeDtypeStruct((MOD_ROWS, n), F32),
        compiler_params=_cparams(("arbitrary",)),
        name="adaln",
    )(cond, w_ada, b_ada)


def _mod_chunk(mod_ref, row, k):
    return mod_ref[pl.ds(row, 1), k * D_MODEL:(k + 1) * D_MODEL]


def _qkv_kernel(x_ref, mod_ref, g_ref, w_ref, qn_ref, kn_ref, cos_ref, sa_ref, sb_ref,
                qg_ref, kg_ref, vg_ref, qw_ref, kw_ref, vw_ref, *, rope, tiles_per_row):
    i = pl.program_id(0)
    row = (1 + i // tiles_per_row) if rope else 0
    x = x_ref[...]
    h = _rms(x, g_ref[...]) * (1.0 + _mod_chunk(mod_ref, row, 1)) + _mod_chunk(mod_ref, row, 0)
    proj = jnp.dot(h.astype(BF16), w_ref[...], preferred_element_type=F32)

    if rope:
        cos, sa, sb = cos_ref[...], sa_ref[...], sb_ref[...]

    def rot(t):
        if not rope:
            return t
        return t * cos + pltpu.roll(t, 96, 1) * sa + pltpu.roll(t, 32, 1) * sb

    scale = HEAD_DIM ** -0.5
    qn, kn = qn_ref[...], kn_ref[...]
    off = 0
    for hd in range(N_HEADS):
        t = proj[:, off + hd * HEAD_DIM: off + (hd + 1) * HEAD_DIM]
        qg_ref[:, hd * HEAD_DIM:(hd + 1) * HEAD_DIM] = (rot(_rms(t, qn)) * scale).astype(qg_ref.dtype)
    off += Q_W
    for hd in range(N_KV):
        t = proj[:, off + hd * HEAD_DIM: off + (hd + 1) * HEAD_DIM]
        kg_ref[:, hd * HEAD_DIM:(hd + 1) * HEAD_DIM] = rot(_rms(t, kn)).astype(kg_ref.dtype)
    off += KV_W
    vg_ref[...] = proj[:, off:off + KV_W].astype(vg_ref.dtype)
    off += KV_W
    for hd in range(N_HEADS):
        t = proj[:, off + hd * HEAD_DIM: off + (hd + 1) * HEAD_DIM]
        qw_ref[:, hd * HEAD_DIM:(hd + 1) * HEAD_DIM] = (rot(t) * scale).astype(qw_ref.dtype)
    off += Q_W
    for hd in range(N_KV):
        t = proj[:, off + hd * HEAD_DIM: off + (hd + 1) * HEAD_DIM]
        kw_ref[:, hd * HEAD_DIM:(hd + 1) * HEAD_DIM] = rot(t).astype(kw_ref.dtype)
    off += KV_W
    vw_ref[...] = proj[:, off:off + KV_W].astype(vw_ref.dtype)


def _qkv(x2d, mod, pre_g, w_in_b, qn, kn, cos, sa, sb, *, rope, seq, kv_dtype):
    t = x2d.shape[0]
    tiles_per_row = seq // TM
    full = lambda shape: pl.BlockSpec(shape, lambda i: (0,) * len(shape))
    tab = pl.BlockSpec((TM, HEAD_DIM), (lambda i: (i % tiles_per_row, 0)) if rope else (lambda i: (0, 0)))
    tok = lambda w: pl.BlockSpec((TM, w), lambda i: (i, 0))
    return pl.pallas_call(
        functools.partial(_qkv_kernel, rope=rope, tiles_per_row=tiles_per_row),
        grid=(t // TM,),
        in_specs=[tok(D_MODEL), full(mod.shape), full((1, D_MODEL)),
                  full((D_MODEL, QKV_WIDTH)), full((1, HEAD_DIM)), full((1, HEAD_DIM)),
                  tab, tab, tab],
        out_specs=[tok(Q_W), tok(KV_W), tok(KV_W), tok(Q_W), tok(KV_W), tok(KV_W)],
        out_shape=[jax.ShapeDtypeStruct((t, Q_W), BF16),
                   jax.ShapeDtypeStruct((t, KV_W), kv_dtype),
                   jax.ShapeDtypeStruct((t, KV_W), kv_dtype),
                   jax.ShapeDtypeStruct((t, Q_W), BF16),
                   jax.ShapeDtypeStruct((t, KV_W), kv_dtype),
                   jax.ShapeDtypeStruct((t, KV_W), kv_dtype)],
        compiler_params=_cparams(("arbitrary",)),
        name="qkv_rope" if rope else "qkv_ctx",
    )(x2d, mod, pre_g, w_in_b, qn, kn, cos, sa, sb)


def _softmax_pv(scores, values, sink):
    m = scores[0].max(axis=-1, keepdims=True)
    for s in scores[1:]:
        m = jnp.maximum(m, s.max(axis=-1, keepdims=True))
    if sink is not None:
        m = jnp.maximum(m, sink)
    den = jnp.exp(sink - m) if sink is not None else 0.0
    acc = None
    for s, v in zip(scores, values):
        p = jnp.exp(s - m)
        den = den + p.sum(axis=-1, keepdims=True)
        pv = jnp.dot(p.astype(BF16), v, preferred_element_type=F32)
        acc = pv if acc is None else acc + pv
    return acc / den


def _ctx_attn_kernel(sink_ref, qg_ref, kg_ref, vg_ref, qw_ref, kw_ref, vw_ref, o_ref):
    for mixer, (q_ref, k_ref, v_ref) in enumerate(((qg_ref, kg_ref, vg_ref), (qw_ref, kw_ref, vw_ref))):
        for n in range(N_KV):
            k = k_ref[:, n * HEAD_DIM:(n + 1) * HEAD_DIM].astype(BF16)
            v = v_ref[:, n * HEAD_DIM:(n + 1) * HEAD_DIM].astype(BF16)
            for g in range(GROUP):
                hd = n * GROUP + g
                q = q_ref[:, hd * HEAD_DIM:(hd + 1) * HEAD_DIM]
                sink = sink_ref[hd] if mixer == 1 else None
                o = _softmax_pv([_dot_nt(q, k)], [v], sink)
                c0 = mixer * Q_W + hd * HEAD_DIM
                o_ref[:, c0:c0 + HEAD_DIM] = o.astype(o_ref.dtype)


def _ctx_attention(sink, qg, kg, vg, qw, kw, vw, *, seq):
    t = qg.shape[0]
    blk = lambda w: pl.BlockSpec((seq, w), lambda b: (b, 0))
    return pl.pallas_call(
        _ctx_attn_kernel,
        grid=(t // seq,),
        in_specs=[pl.BlockSpec(memory_space=pltpu.SMEM),
                  blk(Q_W), blk(KV_W), blk(KV_W), blk(Q_W), blk(KV_W), blk(KV_W)],
        out_specs=blk(2 * Q_W),
        out_shape=jax.ShapeDtypeStruct((t, 2 * Q_W), BF16),
        compiler_params=_cparams(("arbitrary",)),
        name="attn_ctx",
    )(sink, qg, kg, vg, qw, kw, vw)


def _lat_attn_kernel(sink_ref, qg_ref, kg_ref, vg_ref, qw_ref, kw_ref, vw_ref,
                     cgk_ref, cgv_ref, cwk_ref, cwv_ref, o_ref, *, seq):
    qi = pl.program_id(1)
    span = TQ + 2 * WINDOW
    start = pl.multiple_of(jnp.clip(qi * TQ - WINDOW, 0, seq - span), WINDOW)
    qpos = qi * TQ + lax.broadcasted_iota(I32, (TQ, span), 0)
    kpos = start + lax.broadcasted_iota(I32, (TQ, span), 1)
    band_ok = jnp.abs(kpos - qpos) <= WINDOW
    for n in range(N_KV):
        cols = slice(n * HEAD_DIM, (n + 1) * HEAD_DIM)
        kc, vc = cgk_ref[:, cols].astype(BF16), cgv_ref[:, cols].astype(BF16)
        kl, vl = kg_ref[:, cols], vg_ref[:, cols]
        for g in range(GROUP):
            hd = n * GROUP + g
            q = qg_ref[:, hd * HEAD_DIM:(hd + 1) * HEAD_DIM]
            o = _softmax_pv([_dot_nt(q, kc), _dot_nt(q, kl)], [vc, vl], None)
            o_ref[:, hd * HEAD_DIM:(hd + 1) * HEAD_DIM] = o.astype(o_ref.dtype)
        kc, vc = cwk_ref[:, cols].astype(BF16), cwv_ref[:, cols].astype(BF16)
        kl, vl = kw_ref[pl.ds(start, span), cols], vw_ref[pl.ds(start, span), cols]
        for g in range(GROUP):
            hd = n * GROUP + g
            q = qw_ref[:, hd * HEAD_DIM:(hd + 1) * HEAD_DIM]
            band = jnp.where(band_ok, _dot_nt(q, kl), NEG)
            o = _softmax_pv([_dot_nt(q, kc), band], [vc, vl], sink_ref[hd])
            c0 = Q_W + hd * HEAD_DIM
            o_ref[:, c0:c0 + HEAD_DIM] = o.astype(o_ref.dtype)


def _lat_attention(sink, qg, kg, vg, qw, kw, vw, cgk, cgv, cwk, cwv, *, seq, past):
    t = qg.shape[0]
    nq = seq // TQ
    qblk = pl.BlockSpec((TQ, Q_W), lambda b, i: (b * nq + i, 0))
    kblk = pl.BlockSpec((seq, KV_W), lambda b, i: (b, 0))
    cblk = pl.BlockSpec((past, KV_W), lambda b, i: (b, 0))
    return pl.pallas_call(
        functools.partial(_lat_attn_kernel, seq=seq),
        grid=(t // seq, nq),
        in_specs=[pl.BlockSpec(memory_space=pltpu.SMEM),
                  qblk, kblk, kblk, qblk, kblk, kblk, cblk, cblk, cblk, cblk],
        out_specs=pl.BlockSpec((TQ, 2 * Q_W), lambda b, i: (b * nq + i, 0)),
        out_shape=jax.ShapeDtypeStruct((t, 2 * Q_W), BF16),
        compiler_params=_cparams(("arbitrary", "arbitrary")),
        name="attn_lat",
    )(sink, qg, kg, vg, qw, kw, vw, cgk, cgv, cwk, cwv)


def _route(sel, scores, tm):
    per = N_EXPERTS // N_GROUPS
    sel3 = sel.reshape(N_GROUPS, per, tm)
    sc3 = scores.reshape(N_GROUPS, per, tm)
    member = lax.broadcasted_iota(I32, (N_GROUPS, per, tm), 1)
    m1 = sel3.max(axis=1, keepdims=True)
    first = jnp.where(sel3 == m1, member, per).min(axis=1, keepdims=True)
    m2 = jnp.where(member == first, -jnp.inf, sel3).max(axis=1, keepdims=True)
    gs = m1 + m2

    gid = lax.broadcasted_iota(I32, (N_GROUPS, 1, tm), 0)
    chosen = jnp.zeros((N_GROUPS, 1, tm), F32)
    for _ in range(TOPK_GROUPS):
        gm = gs.max(axis=0, keepdims=True)
        gfirst = jnp.where(gs == gm, gid, N_GROUPS).min(axis=0, keepdims=True)
        hit = gid == gfirst
        chosen = jnp.where(hit, 1.0, chosen)
        gs = jnp.where(hit, -jnp.inf, gs)

    eid = lax.broadcasted_iota(I32, (N_GROUPS, per, tm), 0) * per + member
    masked = jnp.where(chosen > 0.0, sel3, -jnp.inf)
    ids, raw = [], []
    for _ in range(TOP_K):
        mx = masked.max(axis=1, keepdims=True).max(axis=0, keepdims=True)
        efirst = jnp.where(masked == mx, eid, N_EXPERTS).min(axis=1, keepdims=True).min(axis=0, keepdims=True)
        hit = eid == efirst
        ids.append(efirst.reshape(1, tm))
        raw.append(jnp.where(hit, sc3, 0.0).sum(axis=1, keepdims=True).sum(axis=0, keepdims=True).reshape(1, tm))
        masked = jnp.where(hit, -jnp.inf, masked)
    ids = jnp.concatenate(ids, axis=0)
    raw = jnp.concatenate(raw, axis=0)
    gates = raw / raw.sum(axis=0, keepdims=True) * ROUTED_SCALE
    return ids, gates


def _post_attn_kernel(op_ref, os_ref, xp_ref, xs_ref, mod_ref, w_ref, pg_ref, fg_ref,
                      wrh_ref, wrl_ref, rb_ref,
                      y_ref, h_ref, ids_ref, gates_ref, *, ctx_tiles, tiles_per_row):
    i = pl.program_id(0)
    is_ctx = i < ctx_tiles
    row = jnp.where(is_ctx, 0, 1 + (i - ctx_tiles) // tiles_per_row)
    o = jnp.where(is_ctx, op_ref[...], os_ref[...])
    x = jnp.where(is_ctx, xp_ref[...], xs_ref[...])
    a = jnp.dot(o, w_ref[...], preferred_element_type=F32)
    y = x + _mod_chunk(mod_ref, row, 2) * _rms(a, pg_ref[...])
    y_ref[...] = y
    h = _rms(y, fg_ref[...]) * (1.0 + _mod_chunk(mod_ref, row, 4)) + _mod_chunk(mod_ref, row, 3)
    _store_rows(h_ref, h.shape[0], _pack_bf16_pair(h[:, :HALF], h[:, HALF:]))
    h_hi = h.astype(BF16)
    h_lo = (h - h_hi.astype(F32)).astype(BF16)
    wh, wl = wrh_ref[...], wrl_ref[...]
    logits = _dot_nt(wh, h_hi) + (_dot_nt(wl, h_hi) + _dot_nt(wh, h_lo))
    scores = 1.0 / (1.0 + jnp.exp(-logits))
    ids, gates = _route(scores + rb_ref[...], scores, logits.shape[1])
    ids_ref[...] = ids
    gates_ref[...] = gates


def _post_attn(o_p, o_s, x_p, x_s, mod, w_out_b, post_g, ffn_g, wr_hi, wr_lo, rbias, *, seq):
    tp, ts = x_p.shape[0], x_s.shape[0]
    t = tp + ts
    ctx_tiles = tp // TM
    full = lambda shape: pl.BlockSpec(shape, lambda i: (0,) * len(shape))
    ctx = lambda w: pl.BlockSpec((TM, w), lambda i: (jnp.minimum(i, ctx_tiles - 1), 0))
    lat = lambda w: pl.BlockSpec((TM, w), lambda i: (jnp.maximum(i - ctx_tiles, 0), 0))
    tok = lambda w: pl.BlockSpec((TM, w), lambda i: (i, 0))
    lane = pl.BlockSpec((TOP_K, TM), lambda i: (0, i))
    return pl.pallas_call(
        functools.partial(_post_attn_kernel, ctx_tiles=ctx_tiles, tiles_per_row=seq // TM),
        grid=(t // TM,),
        in_specs=[ctx(D_MODEL), lat(D_MODEL), ctx(D_MODEL), lat(D_MODEL), full(mod.shape),
                  full((D_MODEL, D_MODEL)), full((1, D_MODEL)), full((1, D_MODEL)),
                  full((N_EXPERTS, D_MODEL)), full((N_EXPERTS, D_MODEL)), full((N_EXPERTS, 1))],
        out_specs=[tok(D_MODEL), pl.BlockSpec((TM * CHUNKS, LANES), lambda i: (i, 0)), lane, lane],
        out_shape=[jax.ShapeDtypeStruct((t, D_MODEL), F32),
                   jax.ShapeDtypeStruct((t * CHUNKS, LANES), U32),
                   jax.ShapeDtypeStruct((TOP_K, t), I32),
                   jax.ShapeDtypeStruct((TOP_K, t), F32)],
        compiler_params=_cparams(("arbitrary",)),
        name="post_attn_router",
    )(o_p, o_s, x_p, x_s, mod, w_out_b, post_g, ffn_g, wr_hi, wr_lo, rbias)


def _stage_indices(idx_vmem, idx_smem, sem):
    cp = pltpu.make_async_copy(idx_vmem.at[0, 0], idx_smem, sem)
    cp.start()
    cp.wait()


def _issue_row_dmas(idx_smem, copy_of):
    def slot_body(j, carry):
        def chunk(c, carry):
            for u in range(ISSUE_UNROLL):
                tok = c * ISSUE_UNROLL + u
                copy_of(j, tok, pl.multiple_of(idx_smem[j * TM + tok], CHUNKS)).start(
                    priority=u % DMA_THREADS)
            return carry
        return lax.fori_loop(0, TM // ISSUE_UNROLL, chunk, carry)
    lax.fori_loop(0, TOP_K, slot_body, 0)


def _dispatch_kernel(idx_ref, h_ref, xs_hbm, idx_smem, isem, dsem):
    _stage_indices(idx_ref, idx_smem, isem)

    def copy_of(j, tok, dst):
        src = pl.multiple_of(tok * CHUNKS, CHUNKS)
        return pltpu.make_async_copy(h_ref.at[pl.ds(src, CHUNKS)], xs_hbm.at[pl.ds(dst, CHUNKS)], dsem)

    _issue_row_dmas(idx_smem, copy_of)
    for _ in range(TOP_K):
        pltpu.make_async_copy(h_ref, xs_hbm.at[pl.ds(0, TM * CHUNKS)], dsem).wait()


def _dispatch(dest_tiles, h_packed):
    n_tiles = dest_tiles.shape[0]
    rows = n_tiles * TM * TOP_K
    return pl.pallas_call(
        _dispatch_kernel,
        grid=(n_tiles,),
        in_specs=[pl.BlockSpec((1, 1, TOP_K * TM), lambda i: (i, 0, 0)),
                  pl.BlockSpec((TM * CHUNKS, LANES), lambda i: (i, 0))],
        out_specs=pl.BlockSpec(memory_space=pl.ANY),
        out_shape=jax.ShapeDtypeStruct((rows * CHUNKS, LANES), U32),
        scratch_shapes=[pltpu.SMEM((TOP_K * TM,), I32),
                        pltpu.SemaphoreType.DMA,
                        pltpu.SemaphoreType.DMA],
        compiler_params=_cparams(("arbitrary",)),
        name="dispatch",
    )(dest_tiles, h_packed)


def _experts_kernel(vb_ref, ve_ref, vlo_ref, vhi_ref, vfirst_ref, nvis_ref,
                    x_ref, wg_ref, wu_ref, wd_ref, out_ref, wgb, wub, wdb):
    v = pl.program_id(0)

    @pl.when(v < nvis_ref[0])
    def _():
        prev = ve_ref[jnp.maximum(v - 1, 0)]

        @pl.when((v == 0) | (ve_ref[v] != prev))
        def _():
            wgb[...] = wg_ref[0].astype(BF16)
            wub[...] = wu_ref[0].astype(BF16)
            wdb[...] = wd_ref[0].astype(BF16)

        @pl.when(vfirst_ref[v] == 1)
        def _():
            out_ref[...] = jnp.zeros_like(out_ref)

        y = _swiglu_packed(_load_rows(x_ref, TE), wgb, wub, wdb)
        rows = lax.broadcasted_iota(I32, (TE, 1), 0)
        mine = (rows >= vlo_ref[v]) & (rows < vhi_ref[v])
        y = jnp.where(mine, _pack_bf16_pair(y[:, :HALF], y[:, HALF:]), _load_rows(out_ref, TE))
        _store_rows(out_ref, TE, y)


def _experts(visits, xs, w_gate_e, w_up_e, w_down_e):
    n_visits = visits[0].shape[0]
    n_rows = xs.shape[0] // CHUNKS
    wspec = lambda shape: pl.BlockSpec((1,) + shape, lambda v, vb, ve, *_: (ve[v], 0, 0))
    rows = pl.BlockSpec((TE * CHUNKS, LANES), lambda v, vb, *_: (vb[v], 0))
    grid_spec = pltpu.PrefetchScalarGridSpec(
        num_scalar_prefetch=6,
        grid=(n_visits,),
        in_specs=[rows, wspec((D_MODEL, D_EXPERT)), wspec((D_MODEL, D_EXPERT)), wspec((D_EXPERT, D_MODEL))],
        out_specs=rows,
        scratch_shapes=[pltpu.VMEM((D_MODEL, D_EXPERT), BF16),
                        pltpu.VMEM((D_MODEL, D_EXPERT), BF16),
                        pltpu.VMEM((D_EXPERT, D_MODEL), BF16)])
    return pl.pallas_call(
        _experts_kernel,
        grid_spec=grid_spec,
        out_shape=jax.ShapeDtypeStruct((n_rows * CHUNKS, LANES), U32),
        compiler_params=_cparams(("arbitrary",)),
        name="experts",
    )(*visits, xs, w_gate_e, w_up_e, w_down_e)


def _ffn_out_kernel(cur_ref, nxt_ref, yb_hbm, gates_ref, h_ref, y1_ref, mod_ref, fg_ref,
                    wg_ref, wu_ref, wd_ref, out_ref, idx_smem, gbuf, ffn_ref, isem, gsem,
                    *, n_tiles, tiles_per_row, mod_row0):
    i = pl.program_id(0)
    slot = i % 2
    n_idx = TOP_K * TM

    def gather(idx_vmem, s):
        _stage_indices(idx_vmem, idx_smem, isem)
        _issue_row_dmas(idx_smem, lambda j, tok, src: pltpu.make_async_copy(
            yb_hbm.at[pl.ds(src, CHUNKS)],
            gbuf.at[s, pl.ds(pl.multiple_of((j * TM + tok) * CHUNKS, CHUNKS), CHUNKS)], gsem.at[s]))

    @pl.when(i == 0)
    def _():
        gather(cur_ref, 0)

    @pl.when(i + 1 < n_tiles)
    def _():
        gather(nxt_ref, 1 - slot)

    shared = _swiglu_packed(_load_rows(h_ref, TM), wg_ref, wu_ref, wd_ref)

    pltpu.make_async_copy(yb_hbm.at[pl.ds(0, n_idx * CHUNKS)], gbuf.at[slot], gsem.at[slot]).wait()
    g = gates_ref[...]
    gate = [jnp.broadcast_to(g[:, j:j + 1], (TM, LANES)) for j in range(TOP_K)]
    rows = gbuf.at[slot]
    for c in range(CHUNKS):
        acc_lo = shared[:, c * LANES:(c + 1) * LANES]
        acc_hi = shared[:, HALF + c * LANES:HALF + (c + 1) * LANES]
        for j in range(TOP_K):
            r_lo, r_hi = _unpack_bf16_pair(rows[pl.ds(j * TM * CHUNKS + c, TM, stride=CHUNKS), :])
            acc_lo = acc_lo + gate[j] * r_lo
            acc_hi = acc_hi + gate[j] * r_hi
        ffn_ref[:, c * LANES:(c + 1) * LANES] = acc_lo
        ffn_ref[:, HALF + c * LANES:HALF + (c + 1) * LANES] = acc_hi
    row = mod_row0 + (i // tiles_per_row if tiles_per_row else 0)
    out_ref[...] = y1_ref[...] + _mod_chunk(mod_ref, row, 5) * _rms(ffn_ref[...], fg_ref[...])


def _ffn_out(dest_tiles, yb, gates_t, h_packed, y1, mod, post_g, wg_b, wu_b, wd_b,
             *, tile0, n_tiles, tiles_per_row, mod_row0):
    full = lambda shape: pl.BlockSpec(shape, lambda i: (0,) * len(shape))
    tok = lambda w: pl.BlockSpec((TM, w), lambda i: (tile0 + i, 0))
    last = tile0 + n_tiles - 1
    idx = lambda step: pl.BlockSpec((1, 1, TOP_K * TM),
                                    lambda i: (jnp.minimum(tile0 + i + step, last), 0, 0))
    return pl.pallas_call(
        functools.partial(_ffn_out_kernel, n_tiles=n_tiles,
                          tiles_per_row=tiles_per_row, mod_row0=mod_row0),
        grid=(n_tiles,),
        in_specs=[idx(0), idx(1), pl.BlockSpec(memory_space=pl.ANY),
                  tok(TOP_K), pl.BlockSpec((TM * CHUNKS, LANES), lambda i: (tile0 + i, 0)),
                  tok(D_MODEL), full(mod.shape), full((1, D_MODEL)),
                  full((D_MODEL, D_EXPERT)), full((D_MODEL, D_EXPERT)), full((D_EXPERT, D_MODEL))],
        out_specs=pl.BlockSpec((TM, D_MODEL), lambda i: (i, 0)),
        out_shape=jax.ShapeDtypeStruct((n_tiles * TM, D_MODEL), F32),
        scratch_shapes=[pltpu.SMEM((TOP_K * TM,), I32),
                        pltpu.VMEM((2, TOP_K * TM * CHUNKS, LANES), U32),
                        pltpu.VMEM((TM, D_MODEL), F32),
                        pltpu.SemaphoreType.DMA,
                        pltpu.SemaphoreType.DMA((2,))],
        compiler_params=_cparams(("arbitrary",)),
        name="ffn_out",
    )(dest_tiles, dest_tiles, yb, gates_t, h_packed, y1, mod, post_g, wg_b, wu_b, wd_b)


def _sorted_rows_kernel(ids_ref, dest_ref, counts_ref, carry):
    p, i = pl.program_id(0), pl.program_id(1)
    tm = ids_ref.shape[1]
    ids = ids_ref[...]
    expert = lax.broadcasted_iota(I32, (N_EXPERTS, tm), 0)
    chose = [ids[j:j + 1, :] == expert for j in range(TOP_K)]
    onehot = sum(jnp.where(c, 1.0, 0.0) for c in chose)
    tile_counts = jnp.broadcast_to(onehot.sum(axis=1, keepdims=True), (N_EXPERTS, LANES))

    @pl.when((p == 0) & (i == 0))
    def _():
        carry[...] = jnp.zeros_like(carry)

    @pl.when(p == 0)
    def _():
        carry[...] += tile_counts

    @pl.when((p == 1) & (i == 0))
    def _():
        counts = carry[...]
        counts_ref[...] = counts
        row = lax.broadcasted_iota(I32, (N_EXPERTS, LANES), 0)
        incl = counts
        shift = 1
        while shift < N_EXPERTS:
            incl = incl + jnp.where(row >= shift, pltpu.roll(incl, shift, 0), 0.0)
            shift *= 2
        carry[...] = incl - counts

    @pl.when(p == 1)
    def _():
        earlier = (lax.broadcasted_iota(I32, (tm, tm), 0) < lax.broadcasted_iota(I32, (tm, tm), 1))
        before = jnp.dot(onehot.astype(BF16), jnp.where(earlier, 1.0, 0.0).astype(BF16),
                         preferred_element_type=F32)
        pos = carry[:, 0:1] + before
        rows = [jnp.where(c, pos, 0.0).sum(axis=0, keepdims=True) for c in chose]
        dest_ref[...] = jnp.concatenate(rows, axis=0).astype(I32)
        carry[...] += tile_counts


def _sorted_rows(ids):
    t = ids.shape[1]
    n_tiles = t // TM
    return pl.pallas_call(
        _sorted_rows_kernel,
        grid=(2, n_tiles),
        in_specs=[pl.BlockSpec((TOP_K, TM), lambda p, i: (0, i))],
        out_specs=[pl.BlockSpec((TOP_K, TM), lambda p, i: (0, i * p)),
                   pl.BlockSpec((N_EXPERTS, LANES), lambda p, i: (0, 0))],
        out_shape=[jax.ShapeDtypeStruct((TOP_K, t), I32),
                   jax.ShapeDtypeStruct((N_EXPERTS, LANES), F32)],
        scratch_shapes=[pltpu.VMEM((N_EXPERTS, LANES), F32)],
        compiler_params=_cparams(("arbitrary", "arbitrary")),
        name="sorted_rows",
    )(ids)


def _routing_tables(ids):
    t = ids.shape[1]
    n_blocks = t * TOP_K // TE
    n_visits = n_blocks + N_EXPERTS - 1
    dest, counts = _sorted_rows(ids)
    counts = counts[:, 0].astype(I32)
    end = jnp.cumsum(counts)
    start = end - counts

    first_blk = start // TE
    n_vis_e = jnp.where(counts > 0, (end - 1) // TE - first_blk + 1, 0)
    vend = jnp.cumsum(n_vis_e)
    vstart = vend - n_vis_e
    n_vis = vend[-1]
    v = jnp.arange(n_visits, dtype=I32)
    valid = v < n_vis
    ve = jnp.minimum((vend[None, :] <= v[:, None]).sum(axis=1), N_EXPERTS - 1).astype(I32)
    ve = jnp.where(valid, ve, ve[n_vis - 1])
    vb = jnp.where(valid, first_blk[ve] + v - vstart[ve], n_blocks - 1).astype(I32)
    lo = jnp.where(valid, jnp.clip(start[ve] - vb * TE, 0, TE), 0).astype(I32)
    hi = jnp.where(valid, jnp.clip(end[ve] - vb * TE, 0, TE), 0).astype(I32)
    vfirst = (valid & ((v == 0) | (vb != jnp.roll(vb, 1)))).astype(I32)
    return dest.astype(I32), (vb, ve, lo, hi, vfirst, n_vis.reshape(1).astype(I32))


def _rope_tables(n_tokens):
    rows = n_tokens // GRID_W
    row = jnp.repeat(jnp.arange(rows, dtype=F32), GRID_W)
    col = jnp.tile(jnp.arange(GRID_W, dtype=F32), rows)
    inv_freq = ROPE_THETA ** (-jnp.arange(0, AXIS_DIM, 2, dtype=F32) / AXIS_DIM)
    ang_r = row[:, None] * inv_freq
    ang_c = col[:, None] * inv_freq
    ang = jnp.concatenate([ang_r, ang_r, ang_c, ang_c], axis=-1)
    cos, sin = jnp.cos(ang), jnp.sin(ang)
    first = (jnp.arange(HEAD_DIM) % AXIS_DIM) < (AXIS_DIM // 2)
    return cos, jnp.where(first, -sin, 0.0), jnp.where(first, 0.0, sin)


def kernel(x_prompt, x_sample, cache_glob_k, cache_glob_v, cache_win_k, cache_win_v, c, c_ctx,
           w_ada, b_ada, attn_pre_g, attn_post_g, w_in, q_norm_g, k_norm_g, sink_logit, w_out,
           ffn_pre_g, ffn_post_g, w_router, router_bias, w_gate_e, w_up_e, w_down_e,
           w_gate_s, w_up_s, w_down_s):
    batch, seq, d = x_prompt.shape
    dec_batch, dec_seq, _ = x_sample.shape
    past = cache_glob_k.shape[2]
    tp, ts = batch * seq, dec_batch * dec_seq
    t = tp + ts
    l = 0

    xp = x_prompt.reshape(tp, d)
    xs = x_sample.reshape(ts, d)
    cond = jnp.concatenate([c_ctx[None, :], c, jnp.zeros((MOD_ROWS - 1 - dec_batch, d), F32)], axis=0)
    mod = _adaln(cond, w_ada[l], b_ada[l][None, :])

    w_in_b = w_in[l].astype(BF16)
    pre_g = attn_pre_g[l][None, :]
    qn, kn = q_norm_g[l][None, :], k_norm_g[l][None, :]
    cos, sa, sb = _rope_tables(dec_seq)
    sink = sink_logit[l]

    qg, kg, vg, qw, kw, vw = _qkv(xp, mod, pre_g, w_in_b, qn, kn, cos, sa, sb,
                                  rope=False, seq=seq, kv_dtype=F32)
    o_p = _ctx_attention(sink, qg, kg, vg, qw, kw, vw, seq=seq)
    new_kv = [a.reshape(batch, 1, seq, N_KV, HEAD_DIM) for a in (kg, vg, kw, vw)]

    qg, kg, vg, qw, kw, vw = _qkv(xs, mod, pre_g, w_in_b, qn, kn, cos, sa, sb,
                                  rope=True, seq=dec_seq, kv_dtype=BF16)
    caches = [a[:, l].reshape(dec_batch * past, KV_W)
              for a in (cache_glob_k, cache_glob_v, cache_win_k, cache_win_v)]
    o_s = _lat_attention(sink, qg, kg, vg, qw, kw, vw, *caches, seq=dec_seq, past=past)

    wr_t = w_router[l].T
    wr_hi = wr_t.astype(BF16)
    wr_lo = (wr_t - wr_hi.astype(F32)).astype(BF16)
    y1, h_packed, ids, gates = _post_attn(
        o_p, o_s, xp, xs, mod, w_out[l].astype(BF16), attn_post_g[l][None, :], ffn_pre_g[l][None, :],
        wr_hi, wr_lo, router_bias[l][:, None], seq=dec_seq)

    dest, visits = _routing_tables(ids)
    dest_tiles = (dest * CHUNKS).reshape(TOP_K, t // TM, TM).transpose(1, 0, 2).reshape(t // TM, 1, TOP_K * TM)
    xs_rows = _dispatch(dest_tiles, h_packed)
    yb = _experts(visits, xs_rows, w_gate_e[l], w_up_e[l], w_down_e[l])

    gates_t = gates.T
    shared = (w_gate_s[l].astype(BF16), w_up_s[l].astype(BF16), w_down_s[l].astype(BF16))
    post_g = ffn_post_g[l][None, :]
    y_p = _ffn_out(dest_tiles, yb, gates_t, h_packed, y1, mod, post_g, *shared,
                   tile0=0, n_tiles=tp // TM, tiles_per_row=0, mod_row0=0)
    y_s = _ffn_out(dest_tiles, yb, gates_t, h_packed, y1, mod, post_g, *shared,
                   tile0=tp // TM, n_tiles=ts // TM, tiles_per_row=dec_seq // TM, mod_row0=1)
    return (y_p.reshape(batch, seq, d), y_s.reshape(dec_batch, dec_seq, d), *new_kv)
```

```python
import functools

import jax
import jax.numpy as jnp
from jax import lax
from jax.experimental import pallas as pl
from jax.experimental.pallas import tpu as pltpu

F32 = jnp.float32
BF16 = jnp.bfloat16
U32 = jnp.uint32
I32 = jnp.int32

D_MODEL = 2048
GRID_W = 64
HEAD_DIM = 128
AXIS_DIM = HEAD_DIM // 2
N_HEADS = 8
N_KV = 2
GROUP = N_HEADS // N_KV
Q_W = N_HEADS * HEAD_DIM
KV_W = N_KV * HEAD_DIM
QKV_WIDTH = 2 * Q_W + 4 * KV_W
WINDOW = 128
ROPE_THETA = 10000.0
NORM_EPS = 1e-6
N_EXPERTS = 64
N_GROUPS = 8
TOPK_GROUPS = 4
TOP_K = 8
D_EXPERT = 512
ROUTED_SCALE = 2.5

HALF = D_MODEL // 2
LANES = 128
CHUNKS = HALF // LANES
MOD_ROWS = 8
VMEM_LIMIT = 56 * 1024 * 1024

TM = 256
TQ = 256
TE = 256
NEG = -1e30
ISSUE_UNROLL = 32
DMA_THREADS = 2


def _cparams(sem):
    return pltpu.CompilerParams(dimension_semantics=sem, vmem_limit_bytes=VMEM_LIMIT)


def _pack_bf16_pair(lo, hi):
    lo_b = lax.bitcast_convert_type(lo.astype(BF16).astype(F32), U32)
    hi_b = lax.bitcast_convert_type(hi.astype(BF16).astype(F32), U32)
    return (hi_b & jnp.uint32(0xFFFF0000)) | (lo_b >> 16)


def _unpack_bf16_pair(p):
    lo = lax.bitcast_convert_type(p << 16, F32)
    hi = lax.bitcast_convert_type(p & jnp.uint32(0xFFFF0000), F32)
    return lo, hi


def _store_rows(ref, n, packed):
    for c in range(CHUNKS):
        ref[pl.ds(c, n, stride=CHUNKS), :] = packed[:, c * LANES:(c + 1) * LANES]


def _load_rows(ref, n, first=0):
    return jnp.concatenate(
        [ref[pl.ds(first * CHUNKS + c, n, stride=CHUNKS), :] for c in range(CHUNKS)], axis=-1)


def _rms(x, gain):
    r = lax.rsqrt(jnp.mean(x * x, axis=-1, keepdims=True) + NORM_EPS)
    return x * r * gain


def _dot_nt(a, b):
    return lax.dot_general(a, b, (((1,), (1,)), ((), ())), preferred_element_type=F32)


def _swiglu_packed(p, wg, wu, wd):
    lo, hi = _unpack_bf16_pair(p)
    lo, hi = lo.astype(BF16), hi.astype(BF16)
    gate = (jnp.dot(lo, wg[:HALF], preferred_element_type=F32)
            + jnp.dot(hi, wg[HALF:], preferred_element_type=F32))
    up = (jnp.dot(lo, wu[:HALF], preferred_element_type=F32)
          + jnp.dot(hi, wu[HALF:], preferred_element_type=F32))
    act = (gate / (1.0 + jnp.exp(-gate)) * up).astype(BF16)
    return jnp.dot(act, wd[...], preferred_element_type=F32)


def _adaln_kernel(cond_ref, w_ref, b_ref, o_ref):
    c = cond_ref[...]
    s = (c / (1.0 + jnp.exp(-c))).astype(BF16)
    o_ref[...] = jnp.dot(s, w_ref[...].astype(BF16), preferred_element_type=F32) + b_ref[...]


def _adaln(cond, w_ada, b_ada):
    n = w_ada.shape[1]
    tn = 1024
    return pl.pallas_call(
        _adaln_kernel,
        grid=(n // tn,),
        in_specs=[pl.BlockSpec((MOD_ROWS, D_MODEL), lambda j: (0, 0)),
                  pl.BlockSpec((D_MODEL, tn), lambda j: (0, j)),
                  pl.BlockSpec((1, tn), lambda j: (0, j))],
        out_specs=pl.BlockSpec((MOD_ROWS, tn), lambda j: (0, j)),
        out_shape=jax.ShapeDtypeStruct((MOD_ROWS, n), F32),
        compiler_params=_cparams(("arbitrary",)),
        name="adaln",
    )(cond, w_ada, b_ada)


def _mod_chunk(mod_ref, row, k):
    return mod_ref[pl.ds(row, 1), k * D_MODEL:(k + 1) * D_MODEL]


def _qkv_kernel(x_ref, mod_ref, g_ref, w_ref, qn_ref, kn_ref, cos_ref, sa_ref, sb_ref,
                qg_ref, kg_ref, vg_ref, qw_ref, kw_ref, vw_ref, *, rope, tiles_per_row):
    i = pl.program_id(0)
    row = (1 + i // tiles_per_row) if rope else 0
    x = x_ref[...]
    h = _rms(x, g_ref[...]) * (1.0 + _mod_chunk(mod_ref, row, 1)) + _mod_chunk(mod_ref, row, 0)
    proj = jnp.dot(h.astype(BF16), w_ref[...], preferred_element_type=F32)

    if rope:
        cos, sa, sb = cos_ref[...], sa_ref[...], sb_ref[...]

    def rot(t):
        if not rope:
            return t
        return t * cos + pltpu.roll(t, 96, 1) * sa + pltpu.roll(t, 32, 1) * sb

    scale = HEAD_DIM ** -0.5
    qn, kn = qn_ref[...], kn_ref[...]
    off = 0
    for hd in range(N_HEADS):
        t = proj[:, off + hd * HEAD_DIM: off + (hd + 1) * HEAD_DIM]
        qg_ref[:, hd * HEAD_DIM:(hd + 1) * HEAD_DIM] = (rot(_rms(t, qn)) * scale).astype(qg_ref.dtype)
    off += Q_W
    for hd in range(N_KV):
        t = proj[:, off + hd * HEAD_DIM: off + (hd + 1) * HEAD_DIM]
        kg_ref[:, hd * HEAD_DIM:(hd + 1) * HEAD_DIM] = rot(_rms(t, kn)).astype(kg_ref.dtype)
    off += KV_W
    vg_ref[...] = proj[:, off:off + KV_W].astype(vg_ref.dtype)
    off += KV_W
    for hd in range(N_HEADS):
        t = proj[:, off + hd * HEAD_DIM: off + (hd + 1) * HEAD_DIM]
        qw_ref[:, hd * HEAD_DIM:(hd + 1) * HEAD_DIM] = (rot(t) * scale).astype(qw_ref.dtype)
    off += Q_W
    for hd in range(N_KV):
        t = proj[:, off + hd * HEAD_DIM: off + (hd + 1) * HEAD_DIM]
        kw_ref[:, hd * HEAD_DIM:(hd + 1) * HEAD_DIM] = rot(t).astype(kw_ref.dtype)
    off += KV_W
    vw_ref[...] = proj[:, off:off + KV_W].astype(vw_ref.dtype)


def _qkv(x2d, mod, pre_g, w_in_b, qn, kn, cos, sa, sb, *, rope, seq, kv_dtype):
    t = x2d.shape[0]
    tiles_per_row = seq // TM
    full = lambda shape: pl.BlockSpec(shape, lambda i: (0,) * len(shape))
    tab = pl.BlockSpec((TM, HEAD_DIM), (lambda i: (i % tiles_per_row, 0)) if rope else (lambda i: (0, 0)))
    tok = lambda w: pl.BlockSpec((TM, w), lambda i: (i, 0))
    return pl.pallas_call(
        functools.partial(_qkv_kernel, rope=rope, tiles_per_row=tiles_per_row),
        grid=(t // TM,),
        in_specs=[tok(D_MODEL), full(mod.shape), full((1, D_MODEL)),
                  full((D_MODEL, QKV_WIDTH)), full((1, HEAD_DIM)), full((1, HEAD_DIM)),
                  tab, tab, tab],
        out_specs=[tok(Q_W), tok(KV_W), tok(KV_W), tok(Q_W), tok(KV_W), tok(KV_W)],
        out_shape=[jax.ShapeDtypeStruct((t, Q_W), BF16),
                   jax.ShapeDtypeStruct((t, KV_W), kv_dtype),
                   jax.ShapeDtypeStruct((t, KV_W), kv_dtype),
                   jax.ShapeDtypeStruct((t, Q_W), BF16),
                   jax.ShapeDtypeStruct((t, KV_W), kv_dtype),
                   jax.ShapeDtypeStruct((t, KV_W), kv_dtype)],
        compiler_params=_cparams(("arbitrary",)),
        name="qkv_rope" if rope else "qkv_ctx",
    )(x2d, mod, pre_g, w_in_b, qn, kn, cos, sa, sb)


def _softmax_pv(scores, values, sink):
    m = scores[0].max(axis=-1, keepdims=True)
    for s in scores[1:]:
        m = jnp.maximum(m, s.max(axis=-1, keepdims=True))
    if sink is not None:
        m = jnp.maximum(m, sink)
    den = jnp.exp(sink - m) if sink is not None else 0.0
    acc = None
    for s, v in zip(scores, values):
        p = jnp.exp(s - m)
        den = den + p.sum(axis=-1, keepdims=True)
        pv = jnp.dot(p.astype(BF16), v, preferred_element_type=F32)
        acc = pv if acc is None else acc + pv
    return acc / den


def _ctx_attn_kernel(sink_ref, qg_ref, kg_ref, vg_ref, qw_ref, kw_ref, vw_ref, o_ref):
    for mixer, (q_ref, k_ref, v_ref) in enumerate(((qg_ref, kg_ref, vg_ref), (qw_ref, kw_ref, vw_ref))):
        for n in range(N_KV):
            k = k_ref[:, n * HEAD_DIM:(n + 1) * HEAD_DIM].astype(BF16)
            v = v_ref[:, n * HEAD_DIM:(n + 1) * HEAD_DIM].astype(BF16)
            for g in range(GROUP):
                hd = n * GROUP + g
                q = q_ref[:, hd * HEAD_DIM:(hd + 1) * HEAD_DIM]
                sink = sink_ref[hd] if mixer == 1 else None
                o = _softmax_pv([_dot_nt(q, k)], [v], sink)
                c0 = mixer * Q_W + hd * HEAD_DIM
                o_ref[:, c0:c0 + HEAD_DIM] = o.astype(o_ref.dtype)


def _ctx_attention(sink, qg, kg, vg, qw, kw, vw, *, seq):
    t = qg.shape[0]
    blk = lambda w: pl.BlockSpec((seq, w), lambda b: (b, 0))
    return pl.pallas_call(
        _ctx_attn_kernel,
        grid=(t // seq,),
        in_specs=[pl.BlockSpec(memory_space=pltpu.SMEM),
                  blk(Q_W), blk(KV_W), blk(KV_W), blk(Q_W), blk(KV_W), blk(KV_W)],
        out_specs=blk(2 * Q_W),
        out_shape=jax.ShapeDtypeStruct((t, 2 * Q_W), BF16),
        compiler_params=_cparams(("arbitrary",)),
        name="attn_ctx",
    )(sink, qg, kg, vg, qw, kw, vw)


def _lat_attn_kernel(sink_ref, qg_ref, kg_ref, vg_ref, qw_ref, kw_ref, vw_ref,
                     cgk_ref, cgv_ref, cwk_ref, cwv_ref, o_ref, *, seq):
    qi = pl.program_id(1)
    span = TQ + 2 * WINDOW
    start = pl.multiple_of(jnp.clip(qi * TQ - WINDOW, 0, seq - span), WINDOW)
    qpos = qi * TQ + lax.broadcasted_iota(I32, (TQ, span), 0)
    kpos = start + lax.broadcasted_iota(I32, (TQ, span), 1)
    band_ok = jnp.abs(kpos - qpos) <= WINDOW
    for n in range(N_KV):
        cols = slice(n * HEAD_DIM, (n + 1) * HEAD_DIM)
        kc, vc = cgk_ref[:, cols].astype(BF16), cgv_ref[:, cols].astype(BF16)
        kl, vl = kg_ref[:, cols], vg_ref[:, cols]
        for g in range(GROUP):
            hd = n * GROUP + g
            q = qg_ref[:, hd * HEAD_DIM:(hd + 1) * HEAD_DIM]
            o = _softmax_pv([_dot_nt(q, kc), _dot_nt(q, kl)], [vc, vl], None)
            o_ref[:, hd * HEAD_DIM:(hd + 1) * HEAD_DIM] = o.astype(o_ref.dtype)
        kc, vc = cwk_ref[:, cols].astype(BF16), cwv_ref[:, cols].astype(BF16)
        kl, vl = kw_ref[pl.ds(start, span), cols], vw_ref[pl.ds(start, span), cols]
        for g in range(GROUP):
            hd = n * GROUP + g
            q = qw_ref[:, hd * HEAD_DIM:(hd + 1) * HEAD_DIM]
            band = jnp.where(band_ok, _dot_nt(q, kl), NEG)
            o = _softmax_pv([_dot_nt(q, kc), band], [vc, vl], sink_ref[hd])
            c0 = Q_W + hd * HEAD_DIM
            o_ref[:, c0:c0 + HEAD_DIM] = o.astype(o_ref.dtype)


def _lat_attention(sink, qg, kg, vg, qw, kw, vw, cgk, cgv, cwk, cwv, *, seq, past):
    t = qg.shape[0]
    nq = seq // TQ
    qblk = pl.BlockSpec((TQ, Q_W), lambda b, i: (b * nq + i, 0))
    kblk = pl.BlockSpec((seq, KV_W), lambda b, i: (b, 0))
    cblk = pl.BlockSpec((past, KV_W), lambda b, i: (b, 0))
    return pl.pallas_call(
        functools.partial(_lat_attn_kernel, seq=seq),
        grid=(t // seq, nq),
        in_specs=[pl.BlockSpec(memory_space=pltpu.SMEM),
                  qblk, kblk, kblk, qblk, kblk, kblk, cblk, cblk, cblk, cblk],
        out_specs=pl.BlockSpec((TQ, 2 * Q_W), lambda b, i: (b * nq + i, 0)),
        out_shape=jax.ShapeDtypeStruct((t, 2 * Q_W), BF16),
        compiler_params=_cparams(("arbitrary", "arbitrary")),
        name="attn_lat",
    )(sink, qg, kg, vg, qw, kw, vw, cgk, cgv, cwk, cwv)


def _route(sel, scores, tm):
    per = N_EXPERTS // N_GROUPS
    sel3 = sel.reshape(N_GROUPS, per, tm)
    sc3 = scores.reshape(N_GROUPS, per, tm)
    member = lax.broadcasted_iota(I32, (N_GROUPS, per, tm), 1)
    m1 = sel3.max(axis=1, keepdims=True)
    first = jnp.where(sel3 == m1, member, per).min(axis=1, keepdims=True)
    m2 = jnp.where(member == first, -jnp.inf, sel3).max(axis=1, keepdims=True)
    gs = m1 + m2

    gid = lax.broadcasted_iota(I32, (N_GROUPS, 1, tm), 0)
    chosen = jnp.zeros((N_GROUPS, 1, tm), F32)
    for _ in range(TOPK_GROUPS):
        gm = gs.max(axis=0, keepdims=True)
        gfirst = jnp.where(gs == gm, gid, N_GROUPS).min(axis=0, keepdims=True)
        hit = gid == gfirst
        chosen = jnp.where(hit, 1.0, chosen)
        gs = jnp.where(hit, -jnp.inf, gs)

    eid = lax.broadcasted_iota(I32, (N_GROUPS, per, tm), 0) * per + member
    masked = jnp.where(chosen > 0.0, sel3, -jnp.inf)
    ids, raw = [], []
    for _ in range(TOP_K):
        mx = masked.max(axis=1, keepdims=True).max(axis=0, keepdims=True)
        efirst = jnp.where(masked == mx, eid, N_EXPERTS).min(axis=1, keepdims=True).min(axis=0, keepdims=True)
        hit = eid == efirst
        ids.append(efirst.reshape(1, tm))
        raw.append(jnp.where(hit, sc3, 0.0).sum(axis=1, keepdims=True).sum(axis=0, keepdims=True).reshape(1, tm))
        masked = jnp.where(hit, -jnp.inf, masked)
    ids = jnp.concatenate(ids, axis=0)
    raw = jnp.concatenate(raw, axis=0)
    gates = raw / raw.sum(axis=0, keepdims=True) * ROUTED_SCALE
    return ids, gates


def _post_attn_kernel(op_ref, os_ref, xp_ref, xs_ref, mod_ref, w_ref, pg_ref, fg_ref,
                      wrh_ref, wrl_ref, rb_ref,
                      y_ref, h_ref, ids_ref, gates_ref, *, ctx_tiles, tiles_per_row):
    i = pl.program_id(0)
    is_ctx = i < ctx_tiles
    row = jnp.where(is_ctx, 0, 1 + (i - ctx_tiles) // tiles_per_row)
    o = jnp.where(is_ctx, op_ref[...], os_ref[...])
    x = jnp.where(is_ctx, xp_ref[...], xs_ref[...])
    a = jnp.dot(o, w_ref[...], preferred_element_type=F32)
    y = x + _mod_chunk(mod_ref, row, 2) * _rms(a, pg_ref[...])
    y_ref[...] = y
    h = _rms(y, fg_ref[...]) * (1.0 + _mod_chunk(mod_ref, row, 4)) + _mod_chunk(mod_ref, row, 3)
    _store_rows(h_ref, h.shape[0], _pack_bf16_pair(h[:, :HALF], h[:, HALF:]))
    h_hi = h.astype(BF16)
    h_lo = (h - h_hi.astype(F32)).astype(BF16)
    wh, wl = wrh_ref[...], wrl_ref[...]
    logits = _dot_nt(wh, h_hi) + (_dot_nt(wl, h_hi) + _dot_nt(wh, h_lo))
    scores = 1.0 / (1.0 + jnp.exp(-logits))
    ids, gates = _route(scores + rb_ref[...], scores, logits.shape[1])
    ids_ref[...] = ids
    gates_ref[...] = gates


def _post_attn(o_p, o_s, x_p, x_s, mod, w_out_b, post_g, ffn_g, wr_hi, wr_lo, rbias, *, seq):
    tp, ts = x_p.shape[0], x_s.shape[0]
    t = tp + ts
    ctx_tiles = tp // TM
    full = lambda shape: pl.BlockSpec(shape, lambda i: (0,) * len(shape))
    ctx = lambda w: pl.BlockSpec((TM, w), lambda i: (jnp.minimum(i, ctx_tiles - 1), 0))
    lat = lambda w: pl.BlockSpec((TM, w), lambda i: (jnp.maximum(i - ctx_tiles, 0), 0))
    tok = lambda w: pl.BlockSpec((TM, w), lambda i: (i, 0))
    lane = pl.BlockSpec((TOP_K, TM), lambda i: (0, i))
    return pl.pallas_call(
        functools.partial(_post_attn_kernel, ctx_tiles=ctx_tiles, tiles_per_row=seq // TM),
        grid=(t // TM,),
        in_specs=[ctx(D_MODEL), lat(D_MODEL), ctx(D_MODEL), lat(D_MODEL), full(mod.shape),
                  full((D_MODEL, D_MODEL)), full((1, D_MODEL)), full((1, D_MODEL)),
                  full((N_EXPERTS, D_MODEL)), full((N_EXPERTS, D_MODEL)), full((N_EXPERTS, 1))],
        out_specs=[tok(D_MODEL), pl.BlockSpec((TM * CHUNKS, LANES), lambda i: (i, 0)), lane, lane],
        out_shape=[jax.ShapeDtypeStruct((t, D_MODEL), F32),
                   jax.ShapeDtypeStruct((t * CHUNKS, LANES), U32),
                   jax.ShapeDtypeStruct((TOP_K, t), I32),
                   jax.ShapeDtypeStruct((TOP_K, t), F32)],
        compiler_params=_cparams(("arbitrary",)),
        name="post_attn_router",
    )(o_p, o_s, x_p, x_s, mod, w_out_b, post_g, ffn_g, wr_hi, wr_lo, rbias)


def _stage_indices(idx_vmem, idx_smem, sem):
    cp = pltpu.make_async_copy(idx_vmem.at[0, 0], idx_smem, sem)
    cp.start()
    cp.wait()


def _issue_row_dmas(idx_smem, copy_of):
    def slot_body(j, carry):
        def chunk(c, carry):
            for u in range(ISSUE_UNROLL):
                tok = c * ISSUE_UNROLL + u
                copy_of(j, tok, pl.multiple_of(idx_smem[j * TM + tok], CHUNKS)).start(
                    priority=u % DMA_THREADS)
            return carry
        return lax.fori_loop(0, TM // ISSUE_UNROLL, chunk, carry)
    lax.fori_loop(0, TOP_K, slot_body, 0)


def _dispatch_kernel(idx_ref, h_ref, xs_hbm, idx_smem, isem, dsem):
    _stage_indices(idx_ref, idx_smem, isem)

    def copy_of(j, tok, dst):
        src = pl.multiple_of(tok * CHUNKS, CHUNKS)
        return pltpu.make_async_copy(h_ref.at[pl.ds(src, CHUNKS)], xs_hbm.at[pl.ds(dst, CHUNKS)], dsem)

    _issue_row_dmas(idx_smem, copy_of)
    for _ in range(TOP_K):
        pltpu.make_async_copy(h_ref, xs_hbm.at[pl.ds(0, TM * CHUNKS)], dsem).wait()


def _dispatch(dest_tiles, h_packed):
    n_tiles = dest_tiles.shape[0]
    rows = n_tiles * TM * TOP_K
    return pl.pallas_call(
        _dispatch_kernel,
        grid=(n_tiles,),
        in_specs=[pl.BlockSpec((1, 1, TOP_K * TM), lambda i: (i, 0, 0)),
                  pl.BlockSpec((TM * CHUNKS, LANES), lambda i: (i, 0))],
        out_specs=pl.BlockSpec(memory_space=pl.ANY),
        out_shape=jax.ShapeDtypeStruct((rows * CHUNKS, LANES), U32),
        scratch_shapes=[pltpu.SMEM((TOP_K * TM,), I32),
                        pltpu.SemaphoreType.DMA,
                        pltpu.SemaphoreType.DMA],
        compiler_params=_cparams(("arbitrary",)),
        name="dispatch",
    )(dest_tiles, h_packed)


def _experts_kernel(vb_ref, ve_ref, vlo_ref, vhi_ref, vfirst_ref, nvis_ref,
                    x_ref, wg_ref, wu_ref, wd_ref, out_ref, wgb, wub, wdb):
    v = pl.program_id(0)

    @pl.when(v < nvis_ref[0])
    def _():
        prev = ve_ref[jnp.maximum(v - 1, 0)]

        @pl.when((v == 0) | (ve_ref[v] != prev))
        def _():
            wgb[...] = wg_ref[0].astype(BF16)
            wub[...] = wu_ref[0].astype(BF16)
            wdb[...] = wd_ref[0].astype(BF16)

        @pl.when(vfirst_ref[v] == 1)
        def _():
            out_ref[...] = jnp.zeros_like(out_ref)

        y = _swiglu_packed(_load_rows(x_ref, TE), wgb, wub, wdb)
        rows = lax.broadcasted_iota(I32, (TE, 1), 0)
        mine = (rows >= vlo_ref[v]) & (rows < vhi_ref[v])
        y = jnp.where(mine, _pack_bf16_pair(y[:, :HALF], y[:, HALF:]), _load_rows(out_ref, TE))
        _store_rows(out_ref, TE, y)


def _experts(visits, xs, w_gate_e, w_up_e, w_down_e):
    n_visits = visits[0].shape[0]
    n_rows = xs.shape[0] // CHUNKS
    wspec = lambda shape: pl.BlockSpec((1,) + shape, lambda v, vb, ve, *_: (ve[v], 0, 0))
    rows = pl.BlockSpec((TE * CHUNKS, LANES), lambda v, vb, *_: (vb[v], 0))
    grid_spec = pltpu.PrefetchScalarGridSpec(
        num_scalar_prefetch=6,
        grid=(n_visits,),
        in_specs=[rows, wspec((D_MODEL, D_EXPERT)), wspec((D_MODEL, D_EXPERT)), wspec((D_EXPERT, D_MODEL))],
        out_specs=rows,
        scratch_shapes=[pltpu.VMEM((D_MODEL, D_EXPERT), BF16),
                        pltpu.VMEM((D_MODEL, D_EXPERT), BF16),
                        pltpu.VMEM((D_EXPERT, D_MODEL), BF16)])
    return pl.pallas_call(
        _experts_kernel,
        grid_spec=grid_spec,
        out_shape=jax.ShapeDtypeStruct((n_rows * CHUNKS, LANES), U32),
        compiler_params=_cparams(("arbitrary",)),
        name="experts",
    )(*visits, xs, w_gate_e, w_up_e, w_down_e)


def _ffn_out_kernel(cur_ref, nxt_ref, yb_hbm, gates_ref, h_ref, y1_ref, mod_ref, fg_ref,
                    wg_ref, wu_ref, wd_ref, out_ref, idx_smem, gbuf, ffn_ref, isem, gsem,
                    *, n_tiles, tiles_per_row, mod_row0):
    i = pl.program_id(0)
    slot = i % 2
    n_idx = TOP_K * TM

    def gather(idx_vmem, s):
        _stage_indices(idx_vmem, idx_smem, isem)
        _issue_row_dmas(idx_smem, lambda j, tok, src: pltpu.make_async_copy(
            yb_hbm.at[pl.ds(src, CHUNKS)],
            gbuf.at[s, pl.ds(pl.multiple_of((j * TM + tok) * CHUNKS, CHUNKS), CHUNKS)], gsem.at[s]))

    @pl.when(i == 0)
    def _():
        gather(cur_ref, 0)

    @pl.when(i + 1 < n_tiles)
    def _():
        gather(nxt_ref, 1 - slot)

    shared = _swiglu_packed(_load_rows(h_ref, TM), wg_ref, wu_ref, wd_ref)

    pltpu.make_async_copy(yb_hbm.at[pl.ds(0, n_idx * CHUNKS)], gbuf.at[slot], gsem.at[slot]).wait()
    g = gates_ref[...]
    gate = [jnp.broadcast_to(g[:, j:j + 1], (TM, LANES)) for j in range(TOP_K)]
    rows = gbuf.at[slot]
    for c in range(CHUNKS):
        acc_lo = shared[:, c * LANES:(c + 1) * LANES]
        acc_hi = shared[:, HALF + c * LANES:HALF + (c + 1) * LANES]
        for j in range(TOP_K):
            r_lo, r_hi = _unpack_bf16_pair(rows[pl.ds(j * TM * CHUNKS + c, TM, stride=CHUNKS), :])
            acc_lo = acc_lo + gate[j] * r_lo
            acc_hi = acc_hi + gate[j] * r_hi
        ffn_ref[:, c * LANES:(c + 1) * LANES] = acc_lo
        ffn_ref[:, HALF + c * LANES:HALF + (c + 1) * LANES] = acc_hi
    row = mod_row0 + (i // tiles_per_row if tiles_per_row else 0)
    out_ref[...] = y1_ref[...] + _mod_chunk(mod_ref, row, 5) * _rms(ffn_ref[...], fg_ref[...])


def _ffn_out(dest_tiles, yb, gates_t, h_packed, y1, mod, post_g, wg_b, wu_b, wd_b,
             *, tile0, n_tiles, tiles_per_row, mod_row0):
    full = lambda shape: pl.BlockSpec(shape, lambda i: (0,) * len(shape))
    tok = lambda w: pl.BlockSpec((TM, w), lambda i: (tile0 + i, 0))
    last = tile0 + n_tiles - 1
    idx = lambda step: pl.BlockSpec((1, 1, TOP_K * TM),
                                    lambda i: (jnp.minimum(tile0 + i + step, last), 0, 0))
    return pl.pallas_call(
        functools.partial(_ffn_out_kernel, n_tiles=n_tiles,
                          tiles_per_row=tiles_per_row, mod_row0=mod_row0),
        grid=(n_tiles,),
        in_specs=[idx(0), idx(1), pl.BlockSpec(memory_space=pl.ANY),
                  tok(TOP_K), pl.BlockSpec((TM * CHUNKS, LANES), lambda i: (tile0 + i, 0)),
                  tok(D_MODEL), full(mod.shape), full((1, D_MODEL)),
                  full((D_MODEL, D_EXPERT)), full((D_MODEL, D_EXPERT)), full((D_EXPERT, D_MODEL))],
        out_specs=pl.BlockSpec((TM, D_MODEL), lambda i: (i, 0)),
        out_shape=jax.ShapeDtypeStruct((n_tiles * TM, D_MODEL), F32),
        scratch_shapes=[pltpu.SMEM((TOP_K * TM,), I32),
                        pltpu.VMEM((2, TOP_K * TM * CHUNKS, LANES), U32),
                        pltpu.VMEM((TM, D_MODEL), F32),
                        pltpu.SemaphoreType.DMA,
                        pltpu.SemaphoreType.DMA((2,))],
        compiler_params=_cparams(("arbitrary",)),
        name="ffn_out",
    )(dest_tiles, dest_tiles, yb, gates_t, h_packed, y1, mod, post_g, wg_b, wu_b, wd_b)


def _sorted_rows_kernel(ids_ref, dest_ref, counts_ref, carry):
    p, i = pl.program_id(0), pl.program_id(1)
    tm = ids_ref.shape[1]
    ids = ids_ref[...]
    expert = lax.broadcasted_iota(I32, (N_EXPERTS, tm), 0)
    chose = [ids[j:j + 1, :] == expert for j in range(TOP_K)]
    onehot = sum(jnp.where(c, 1.0, 0.0) for c in chose)
    tile_counts = jnp.broadcast_to(onehot.sum(axis=1, keepdims=True), (N_EXPERTS, LANES))

    @pl.when((p == 0) & (i == 0))
    def _():
        carry[...] = jnp.zeros_like(carry)

    @pl.when(p == 0)
    def _():
        carry[...] += tile_counts

    @pl.when((p == 1) & (i == 0))
    def _():
        counts = carry[...]
        counts_ref[...] = counts
        row = lax.broadcasted_iota(I32, (N_EXPERTS, LANES), 0)
        incl = counts
        shift = 1
        while shift < N_EXPERTS:
            incl = incl + jnp.where(row >= shift, pltpu.roll(incl, shift, 0), 0.0)
            shift *= 2
        carry[...] = incl - counts

    @pl.when(p == 1)
    def _():
        earlier = (lax.broadcasted_iota(I32, (tm, tm), 0) < lax.broadcasted_iota(I32, (tm, tm), 1))
        before = jnp.dot(onehot.astype(BF16), jnp.where(earlier, 1.0, 0.0).astype(BF16),
                         preferred_element_type=F32)
        pos = carry[:, 0:1] + before
        rows = [jnp.where(c, pos, 0.0).sum(axis=0, keepdims=True) for c in chose]
        dest_ref[...] = jnp.concatenate(rows, axis=0).astype(I32)
        carry[...] += tile_counts


def _sorted_rows(ids):
    t = ids.shape[1]
    n_tiles = t // TM
    return pl.pallas_call(
        _sorted_rows_kernel,
        grid=(2, n_tiles),
        in_specs=[pl.BlockSpec((TOP_K, TM), lambda p, i: (0, i))],
        out_specs=[pl.BlockSpec((TOP_K, TM), lambda p, i: (0, i * p)),
                   pl.BlockSpec((N_EXPERTS, LANES), lambda p, i: (0, 0))],
        out_shape=[jax.ShapeDtypeStruct((TOP_K, t), I32),
                   jax.ShapeDtypeStruct((N_EXPERTS, LANES), F32)],
        scratch_shapes=[pltpu.VMEM((N_EXPERTS, LANES), F32)],
        compiler_params=_cparams(("arbitrary", "arbitrary")),
        name="sorted_rows",
    )(ids)


def _routing_tables(ids):
    t = ids.shape[1]
    n_blocks = t * TOP_K // TE
    n_visits = n_blocks + N_EXPERTS - 1
    dest, counts = _sorted_rows(ids)
    counts = counts[:, 0].astype(I32)
    end = jnp.cumsum(counts)
    start = end - counts

    first_blk = start // TE
    n_vis_e = jnp.where(counts > 0, (end - 1) // TE - first_blk + 1, 0)
    vend = jnp.cumsum(n_vis_e)
    vstart = vend - n_vis_e
    n_vis = vend[-1]
    v = jnp.arange(n_visits, dtype=I32)
    valid = v < n_vis
    mine = (vstart[None, :] <= v[:, None]) & (v[:, None] < vend[None, :])
    pick = lambda x: jnp.where(mine, x[None, :], 0).sum(axis=1)
    experts = jnp.arange(N_EXPERTS, dtype=I32)
    last_expert = jnp.max(jnp.where(n_vis_e > 0, experts, 0))
    ve = jnp.where(valid, pick(experts), last_expert).astype(I32)
    vb = jnp.where(valid, pick(first_blk) + v - pick(vstart), n_blocks - 1).astype(I32)
    lo = jnp.where(valid, jnp.clip(pick(start) - vb * TE, 0, TE), 0).astype(I32)
    hi = jnp.where(valid, jnp.clip(pick(end) - vb * TE, 0, TE), 0).astype(I32)
    vfirst = (valid & ((v == 0) | (vb != jnp.roll(vb, 1)))).astype(I32)
    return dest.astype(I32), (vb, ve, lo, hi, vfirst, n_vis.reshape(1).astype(I32))


def _rope_tables(n_tokens):
    rows = n_tokens // GRID_W
    row = jnp.repeat(jnp.arange(rows, dtype=F32), GRID_W)
    col = jnp.tile(jnp.arange(GRID_W, dtype=F32), rows)
    inv_freq = ROPE_THETA ** (-jnp.arange(0, AXIS_DIM, 2, dtype=F32) / AXIS_DIM)
    ang_r = row[:, None] * inv_freq
    ang_c = col[:, None] * inv_freq
    ang = jnp.concatenate([ang_r, ang_r, ang_c, ang_c], axis=-1)
    cos, sin = jnp.cos(ang), jnp.sin(ang)
    first = (jnp.arange(HEAD_DIM) % AXIS_DIM) < (AXIS_DIM // 2)
    return cos, jnp.where(first, -sin, 0.0), jnp.where(first, 0.0, sin)


def kernel(x_prompt, x_sample, cache_glob_k, cache_glob_v, cache_win_k, cache_win_v, c, c_ctx,
           w_ada, b_ada, attn_pre_g, attn_post_g, w_in, q_norm_g, k_norm_g, sink_logit, w_out,
           ffn_pre_g, ffn_post_g, w_router, router_bias, w_gate_e, w_up_e, w_down_e,
           w_gate_s, w_up_s, w_down_s):
    batch, seq, d = x_prompt.shape
    dec_batch, dec_seq, _ = x_sample.shape
    past = cache_glob_k.shape[2]
    tp, ts = batch * seq, dec_batch * dec_seq
    t = tp + ts
    l = 0

    xp = x_prompt.reshape(tp, d)
    xs = x_sample.reshape(ts, d)
    cond = jnp.concatenate([c_ctx[None, :], c, jnp.zeros((MOD_ROWS - 1 - dec_batch, d), F32)], axis=0)
    mod = _adaln(cond, w_ada[l], b_ada[l][None, :])

    w_in_b = w_in[l].astype(BF16)
    pre_g = attn_pre_g[l][None, :]
    qn, kn = q_norm_g[l][None, :], k_norm_g[l][None, :]
    cos, sa, sb = _rope_tables(dec_seq)
    sink = sink_logit[l]

    qg, kg, vg, qw, kw, vw = _qkv(xp, mod, pre_g, w_in_b, qn, kn, cos, sa, sb,
                                  rope=False, seq=seq, kv_dtype=F32)
    o_p = _ctx_attention(sink, qg, kg, vg, qw, kw, vw, seq=seq)
    new_kv = [a.reshape(batch, 1, seq, N_KV, HEAD_DIM) for a in (kg, vg, kw, vw)]

    qg, kg, vg, qw, kw, vw = _qkv(xs, mod, pre_g, w_in_b, qn, kn, cos, sa, sb,
                                  rope=True, seq=dec_seq, kv_dtype=BF16)
    caches = [a[:, l].reshape(dec_batch * past, KV_W)
              for a in (cache_glob_k, cache_glob_v, cache_win_k, cache_win_v)]
    o_s = _lat_attention(sink, qg, kg, vg, qw, kw, vw, *caches, seq=dec_seq, past=past)

    wr_t = w_router[l].T
    wr_hi = wr_t.astype(BF16)
    wr_lo = (wr_t - wr_hi.astype(F32)).astype(BF16)
    y1, h_packed, ids, gates = _post_attn(
        o_p, o_s, xp, xs, mod, w_out[l].astype(BF16), attn_post_g[l][None, :], ffn_pre_g[l][None, :],
        wr_hi, wr_lo, router_bias[l][:, None], seq=dec_seq)

    dest, visits = _routing_tables(ids)
    dest_tiles = (dest * CHUNKS).reshape(TOP_K, t // TM, TM).transpose(1, 0, 2).reshape(t // TM, 1, TOP_K * TM)
    xs_rows = _dispatch(dest_tiles, h_packed)
    yb = _experts(visits, xs_rows, w_gate_e[l], w_up_e[l], w_down_e[l])

    gates_t = gates.T
    shared = (w_gate_s[l].astype(BF16), w_up_s[l].astype(BF16), w_down_s[l].astype(BF16))
    post_g = ffn_post_g[l][None, :]
    y_p = _ffn_out(dest_tiles, yb, gates_t, h_packed, y1, mod, post_g, *shared,
                   tile0=0, n_tiles=tp // TM, tiles_per_row=0, mod_row0=0)
    y_s = _ffn_out(dest_tiles, yb, gates_t, h_packed, y1, mod, post_g, *shared,
                   tile0=tp // TM, n_tiles=ts // TM, tiles_per_row=dec_seq // TM, mod_row0=1)
    return (y_p.reshape(batch, seq, d), y_s.reshape(dec_batch, dec_seq, d), *new_kv)
```

```python
import functools

import jax
import jax.numpy as jnp
from jax import lax
from jax.experimental import pallas as pl
from jax.experimental.pallas import tpu as pltpu

F32 = jnp.float32
BF16 = jnp.bfloat16
U32 = jnp.uint32
I32 = jnp.int32

D_MODEL = 2048
GRID_W = 64
HEAD_DIM = 128
AXIS_DIM = HEAD_DIM // 2
N_HEADS = 8
N_KV = 2
GROUP = N_HEADS // N_KV
Q_W = N_HEADS * HEAD_DIM
KV_W = N_KV * HEAD_DIM
QKV_WIDTH = 2 * Q_W + 4 * KV_W
WINDOW = 128
ROPE_THETA = 10000.0
NORM_EPS = 1e-6
N_EXPERTS = 64
N_GROUPS = 8
TOPK_GROUPS = 4
TOP_K = 8
D_EXPERT = 512
ROUTED_SCALE = 2.5

HALF = D_MODEL // 2
LANES = 128
CHUNKS = HALF // LANES
MOD_ROWS = 8
VMEM_LIMIT = 56 * 1024 * 1024

TM = 256
TQ = 256
TE = 256
NEG = -1e30
ISSUE_UNROLL = 32
DMA_THREADS = 2


def _cparams(sem):
    return pltpu.CompilerParams(dimension_semantics=sem, vmem_limit_bytes=VMEM_LIMIT)


def _pack_bf16_pair(lo, hi):
    lo_b = lax.bitcast_convert_type(lo.astype(BF16).astype(F32), U32)
    hi_b = lax.bitcast_convert_type(hi.astype(BF16).astype(F32), U32)
    return (hi_b & jnp.uint32(0xFFFF0000)) | (lo_b >> 16)


def _unpack_bf16_pair(p):
    lo = lax.bitcast_convert_type(p << 16, F32)
    hi = lax.bitcast_convert_type(p & jnp.uint32(0xFFFF0000), F32)
    return lo, hi


def _store_rows(ref, n, packed):
    for c in range(CHUNKS):
        ref[pl.ds(c, n, stride=CHUNKS), :] = packed[:, c * LANES:(c + 1) * LANES]


def _load_rows(ref, n, first=0):
    return jnp.concatenate(
        [ref[pl.ds(first * CHUNKS + c, n, stride=CHUNKS), :] for c in range(CHUNKS)], axis=-1)


def _rms(x, gain):
    r = lax.rsqrt(jnp.mean(x * x, axis=-1, keepdims=True) + NORM_EPS)
    return x * r * gain


def _dot_nt(a, b):
    return lax.dot_general(a, b, (((1,), (1,)), ((), ())), preferred_element_type=F32)


def _swiglu_packed(p, wg, wu, wd):
    lo, hi = _unpack_bf16_pair(p)
    lo, hi = lo.astype(BF16), hi.astype(BF16)
    gate = (jnp.dot(lo, wg[:HALF], preferred_element_type=F32)
            + jnp.dot(hi, wg[HALF:], preferred_element_type=F32))
    up = (jnp.dot(lo, wu[:HALF], preferred_element_type=F32)
          + jnp.dot(hi, wu[HALF:], preferred_element_type=F32))
    act = (gate / (1.0 + jnp.exp(-gate)) * up).astype(BF16)
    return jnp.dot(act, wd[...], preferred_element_type=F32)


def _adaln_kernel(cond_ref, w_ref, b_ref, o_ref):
    c = cond_ref[...]
    s = (c / (1.0 + jnp.exp(-c))).astype(BF16)
    o_ref[...] = jnp.dot(s, w_ref[...].astype(BF16), preferred_element_type=F32) + b_ref[...]


def _adaln(cond, w_ada, b_ada):
    n = w_ada.shape[1]
    tn = 1024
    return pl.pallas_call(
        _adaln_kernel,
        grid=(n // tn,),
        in_specs=[pl.BlockSpec((MOD_ROWS, D_MODEL), lambda j: (0, 0)),
                  pl.BlockSpec((D_MODEL, tn), lambda j: (0, j)),
                  pl.BlockSpec((1, tn), lambda j: (0, j))],
        out_specs=pl.BlockSpec((MOD_ROWS, tn), lambda j: (0, j)),
        out_shape=jax.ShapeDtypeStruct((MOD_ROWS, n), F32),
        compiler_params=_cparams(("arbitrary",)),
        name="adaln",
    )(cond, w_ada, b_ada)


def _mod_chunk(mod_ref, row, k):
    return mod_ref[pl.ds(row, 1), k * D_MODEL:(k + 1) * D_MODEL]


def _qkv_kernel(x_ref, mod_ref, g_ref, w_ref, qn_ref, kn_ref, cos_ref, sa_ref, sb_ref,
                qg_ref, kg_ref, vg_ref, qw_ref, kw_ref, vw_ref, *, rope, tiles_per_row):
    i = pl.program_id(0)
    row = (1 + i // tiles_per_row) if rope else 0
    x = x_ref[...]
    h = _rms(x, g_ref[...]) * (1.0 + _mod_chunk(mod_ref, row, 1)) + _mod_chunk(mod_ref, row, 0)
    proj = jnp.dot(h.astype(BF16), w_ref[...], preferred_element_type=F32)

    if rope:
        cos, sa, sb = cos_ref[...], sa_ref[...], sb_ref[...]

    def rot(t):
        if not rope:
            return t
        return t * cos + pltpu.roll(t, 96, 1) * sa + pltpu.roll(t, 32, 1) * sb

    scale = HEAD_DIM ** -0.5
    qn, kn = qn_ref[...], kn_ref[...]
    off = 0
    for hd in range(N_HEADS):
        t = proj[:, off + hd * HEAD_DIM: off + (hd + 1) * HEAD_DIM]
        qg_ref[:, hd * HEAD_DIM:(hd + 1) * HEAD_DIM] = (rot(_rms(t, qn)) * scale).astype(qg_ref.dtype)
    off += Q_W
    for hd in range(N_KV):
        t = proj[:, off + hd * HEAD_DIM: off + (hd + 1) * HEAD_DIM]
        kg_ref[:, hd * HEAD_DIM:(hd + 1) * HEAD_DIM] = rot(_rms(t, kn)).astype(kg_ref.dtype)
    off += KV_W
    vg_ref[...] = proj[:, off:off + KV_W].astype(vg_ref.dtype)
    off += KV_W
    for hd in range(N_HEADS):
        t = proj[:, off + hd * HEAD_DIM: off + (hd + 1) * HEAD_DIM]
        qw_ref[:, hd * HEAD_DIM:(hd + 1) * HEAD_DIM] = (rot(t) * scale).astype(qw_ref.dtype)
    off += Q_W
    for hd in range(N_KV):
        t = proj[:, off + hd * HEAD_DIM: off + (hd + 1) * HEAD_DIM]
        kw_ref[:, hd * HEAD_DIM:(hd + 1) * HEAD_DIM] = rot(t).astype(kw_ref.dtype)
    off += KV_W
    vw_ref[...] = proj[:, off:off + KV_W].astype(vw_ref.dtype)


def _qkv(x2d, mod, pre_g, w_in_b, qn, kn, cos, sa, sb, *, rope, seq, kv_dtype):
    t = x2d.shape[0]
    tiles_per_row = seq // TM
    full = lambda shape: pl.BlockSpec(shape, lambda i: (0,) * len(shape))
    tab = pl.BlockSpec((TM, HEAD_DIM), (lambda i: (i % tiles_per_row, 0)) if rope else (lambda i: (0, 0)))
    tok = lambda w: pl.BlockSpec((TM, w), lambda i: (i, 0))
    return pl.pallas_call(
        functools.partial(_qkv_kernel, rope=rope, tiles_per_row=tiles_per_row),
        grid=(t // TM,),
        in_specs=[tok(D_MODEL), full(mod.shape), full((1, D_MODEL)),
                  full((D_MODEL, QKV_WIDTH)), full((1, HEAD_DIM)), full((1, HEAD_DIM)),
                  tab, tab, tab],
        out_specs=[tok(Q_W), tok(KV_W), tok(KV_W), tok(Q_W), tok(KV_W), tok(KV_W)],
        out_shape=[jax.ShapeDtypeStruct((t, Q_W), BF16),
                   jax.ShapeDtypeStruct((t, KV_W), kv_dtype),
                   jax.ShapeDtypeStruct((t, KV_W), kv_dtype),
                   jax.ShapeDtypeStruct((t, Q_W), BF16),
                   jax.ShapeDtypeStruct((t, KV_W), kv_dtype),
                   jax.ShapeDtypeStruct((t, KV_W), kv_dtype)],
        compiler_params=_cparams(("arbitrary",)),
        name="qkv_rope" if rope else "qkv_ctx",
    )(x2d, mod, pre_g, w_in_b, qn, kn, cos, sa, sb)


def _softmax_pv(scores, values, sink):
    m = scores[0].max(axis=-1, keepdims=True)
    for s in scores[1:]:
        m = jnp.maximum(m, s.max(axis=-1, keepdims=True))
    if sink is not None:
        m = jnp.maximum(m, sink)
    den = jnp.exp(sink - m) if sink is not None else 0.0
    acc = None
    for s, v in zip(scores, values):
        p = jnp.exp(s - m)
        den = den + p.sum(axis=-1, keepdims=True)
        pv = jnp.dot(p.astype(BF16), v, preferred_element_type=F32)
        acc = pv if acc is None else acc + pv
    return acc / den


def _ctx_attn_kernel(sink_ref, qg_ref, kg_ref, vg_ref, qw_ref, kw_ref, vw_ref, o_ref):
    for mixer, (q_ref, k_ref, v_ref) in enumerate(((qg_ref, kg_ref, vg_ref), (qw_ref, kw_ref, vw_ref))):
        for n in range(N_KV):
            k = k_ref[:, n * HEAD_DIM:(n + 1) * HEAD_DIM].astype(BF16)
            v = v_ref[:, n * HEAD_DIM:(n + 1) * HEAD_DIM].astype(BF16)
            for g in range(GROUP):
                hd = n * GROUP + g
                q = q_ref[:, hd * HEAD_DIM:(hd + 1) * HEAD_DIM]
                sink = sink_ref[hd] if mixer == 1 else None
                o = _softmax_pv([_dot_nt(q, k)], [v], sink)
                c0 = mixer * Q_W + hd * HEAD_DIM
                o_ref[:, c0:c0 + HEAD_DIM] = o.astype(o_ref.dtype)


def _ctx_attention(sink, qg, kg, vg, qw, kw, vw, *, seq):
    t = qg.shape[0]
    blk = lambda w: pl.BlockSpec((seq, w), lambda b: (b, 0))
    return pl.pallas_call(
        _ctx_attn_kernel,
        grid=(t // seq,),
        in_specs=[pl.BlockSpec(memory_space=pltpu.SMEM),
                  blk(Q_W), blk(KV_W), blk(KV_W), blk(Q_W), blk(KV_W), blk(KV_W)],
        out_specs=blk(2 * Q_W),
        out_shape=jax.ShapeDtypeStruct((t, 2 * Q_W), BF16),
        compiler_params=_cparams(("arbitrary",)),
        name="attn_ctx",
    )(sink, qg, kg, vg, qw, kw, vw)


def _lat_attn_kernel(sink_ref, qg_ref, kg_ref, vg_ref, qw_ref, kw_ref, vw_ref,
                     cgk_ref, cgv_ref, cwk_ref, cwv_ref, o_ref, *, seq):
    qi = pl.program_id(1)
    span = TQ + 2 * WINDOW
    start = pl.multiple_of(jnp.clip(qi * TQ - WINDOW, 0, seq - span), WINDOW)
    qpos = qi * TQ + lax.broadcasted_iota(I32, (TQ, span), 0)
    kpos = start + lax.broadcasted_iota(I32, (TQ, span), 1)
    band_ok = jnp.abs(kpos - qpos) <= WINDOW
    for n in range(N_KV):
        cols = slice(n * HEAD_DIM, (n + 1) * HEAD_DIM)
        kc, vc = cgk_ref[:, cols].astype(BF16), cgv_ref[:, cols].astype(BF16)
        kl, vl = kg_ref[:, cols], vg_ref[:, cols]
        for g in range(GROUP):
            hd = n * GROUP + g
            q = qg_ref[:, hd * HEAD_DIM:(hd + 1) * HEAD_DIM]
            o = _softmax_pv([_dot_nt(q, kc), _dot_nt(q, kl)], [vc, vl], None)
            o_ref[:, hd * HEAD_DIM:(hd + 1) * HEAD_DIM] = o.astype(o_ref.dtype)
        kc, vc = cwk_ref[:, cols].astype(BF16), cwv_ref[:, cols].astype(BF16)
        kl, vl = kw_ref[pl.ds(start, span), cols], vw_ref[pl.ds(start, span), cols]
        for g in range(GROUP):
            hd = n * GROUP + g
            q = qw_ref[:, hd * HEAD_DIM:(hd + 1) * HEAD_DIM]
            band = jnp.where(band_ok, _dot_nt(q, kl), NEG)
            o = _softmax_pv([_dot_nt(q, kc), band], [vc, vl], sink_ref[hd])
            c0 = Q_W + hd * HEAD_DIM
            o_ref[:, c0:c0 + HEAD_DIM] = o.astype(o_ref.dtype)


def _lat_attention(sink, qg, kg, vg, qw, kw, vw, cgk, cgv, cwk, cwv, *, seq, past):
    t = qg.shape[0]
    nq = seq // TQ
    qblk = pl.BlockSpec((TQ, Q_W), lambda b, i: (b * nq + i, 0))
    kblk = pl.BlockSpec((seq, KV_W), lambda b, i: (b, 0))
    cblk = pl.BlockSpec((past, KV_W), lambda b, i: (b, 0))
    return pl.pallas_call(
        functools.partial(_lat_attn_kernel, seq=seq),
        grid=(t // seq, nq),
        in_specs=[pl.BlockSpec(memory_space=pltpu.SMEM),
                  qblk, kblk, kblk, qblk, kblk, kblk, cblk, cblk, cblk, cblk],
        out_specs=pl.BlockSpec((TQ, 2 * Q_W), lambda b, i: (b * nq + i, 0)),
        out_shape=jax.ShapeDtypeStruct((t, 2 * Q_W), BF16),
        compiler_params=_cparams(("arbitrary", "arbitrary")),
        name="attn_lat",
    )(sink, qg, kg, vg, qw, kw, vw, cgk, cgv, cwk, cwv)


def _route(sel, scores, tm):
    per = N_EXPERTS // N_GROUPS
    sel3 = sel.reshape(N_GROUPS, per, tm)
    sc3 = scores.reshape(N_GROUPS, per, tm)
    member = lax.broadcasted_iota(I32, (N_GROUPS, per, tm), 1)
    m1 = sel3.max(axis=1, keepdims=True)
    first = jnp.where(sel3 == m1, member, per).min(axis=1, keepdims=True)
    m2 = jnp.where(member == first, -jnp.inf, sel3).max(axis=1, keepdims=True)
    gs = m1 + m2

    gid = lax.broadcasted_iota(I32, (N_GROUPS, 1, tm), 0)
    chosen = jnp.zeros((N_GROUPS, 1, tm), F32)
    for _ in range(TOPK_GROUPS):
        gm = gs.max(axis=0, keepdims=True)
        gfirst = jnp.where(gs == gm, gid, N_GROUPS).min(axis=0, keepdims=True)
        hit = gid == gfirst
        chosen = jnp.where(hit, 1.0, chosen)
        gs = jnp.where(hit, -jnp.inf, gs)

    eid = lax.broadcasted_iota(I32, (N_GROUPS, per, tm), 0) * per + member
    masked = jnp.where(chosen > 0.0, sel3, -jnp.inf)
    ids, raw = [], []
    for _ in range(TOP_K):
        mx = masked.max(axis=1, keepdims=True).max(axis=0, keepdims=True)
        efirst = jnp.where(masked == mx, eid, N_EXPERTS).min(axis=1, keepdims=True).min(axis=0, keepdims=True)
        hit = eid == efirst
        ids.append(efirst.reshape(1, tm))
        raw.append(jnp.where(hit, sc3, 0.0).sum(axis=1, keepdims=True).sum(axis=0, keepdims=True).reshape(1, tm))
        masked = jnp.where(hit, -jnp.inf, masked)
    ids = jnp.concatenate(ids, axis=0)
    raw = jnp.concatenate(raw, axis=0)
    gates = raw / raw.sum(axis=0, keepdims=True) * ROUTED_SCALE
    return ids, gates


def _post_attn_kernel(op_ref, os_ref, xp_ref, xs_ref, mod_ref, w_ref, pg_ref, fg_ref,
                      wrh_ref, wrl_ref, rb_ref,
                      y_ref, h_ref, ids_ref, gates_ref, *, ctx_tiles, tiles_per_row):
    i = pl.program_id(0)
    is_ctx = i < ctx_tiles
    row = jnp.where(is_ctx, 0, 1 + (i - ctx_tiles) // tiles_per_row)
    o = jnp.where(is_ctx, op_ref[...], os_ref[...])
    x = jnp.where(is_ctx, xp_ref[...], xs_ref[...])
    a = jnp.dot(o, w_ref[...], preferred_element_type=F32)
    y = x + _mod_chunk(mod_ref, row, 2) * _rms(a, pg_ref[...])
    y_ref[...] = y
    h = _rms(y, fg_ref[...]) * (1.0 + _mod_chunk(mod_ref, row, 4)) + _mod_chunk(mod_ref, row, 3)
    _store_rows(h_ref, h.shape[0], _pack_bf16_pair(h[:, :HALF], h[:, HALF:]))
    h_hi = h.astype(BF16)
    h_lo = (h - h_hi.astype(F32)).astype(BF16)
    wh, wl = wrh_ref[...], wrl_ref[...]
    logits = _dot_nt(wh, h_hi) + (_dot_nt(wl, h_hi) + _dot_nt(wh, h_lo))
    scores = 1.0 / (1.0 + jnp.exp(-logits))
    ids, gates = _route(scores + rb_ref[...], scores, logits.shape[1])
    ids_ref[...] = ids
    gates_ref[...] = gates


def _post_attn(o_p, o_s, x_p, x_s, mod, w_out_b, post_g, ffn_g, wr_hi, wr_lo, rbias, *, seq):
    tp, ts = x_p.shape[0], x_s.shape[0]
    t = tp + ts
    ctx_tiles = tp // TM
    full = lambda shape: pl.BlockSpec(shape, lambda i: (0,) * len(shape))
    ctx = lambda w: pl.BlockSpec((TM, w), lambda i: (jnp.minimum(i, ctx_tiles - 1), 0))
    lat = lambda w: pl.BlockSpec((TM, w), lambda i: (jnp.maximum(i - ctx_tiles, 0), 0))
    tok = lambda w: pl.BlockSpec((TM, w), lambda i: (i, 0))
    lane = pl.BlockSpec((TOP_K, TM), lambda i: (0, i))
    return pl.pallas_call(
        functools.partial(_post_attn_kernel, ctx_tiles=ctx_tiles, tiles_per_row=seq // TM),
        grid=(t // TM,),
        in_specs=[ctx(D_MODEL), lat(D_MODEL), ctx(D_MODEL), lat(D_MODEL), full(mod.shape),
                  full((D_MODEL, D_MODEL)), full((1, D_MODEL)), full((1, D_MODEL)),
                  full((N_EXPERTS, D_MODEL)), full((N_EXPERTS, D_MODEL)), full((N_EXPERTS, 1))],
        out_specs=[tok(D_MODEL), pl.BlockSpec((TM * CHUNKS, LANES), lambda i: (i, 0)), lane, lane],
        out_shape=[jax.ShapeDtypeStruct((t, D_MODEL), F32),
                   jax.ShapeDtypeStruct((t * CHUNKS, LANES), U32),
                   jax.ShapeDtypeStruct((TOP_K, t), I32),
                   jax.ShapeDtypeStruct((TOP_K, t), F32)],
        compiler_params=_cparams(("arbitrary",)),
        name="post_attn_router",
    )(o_p, o_s, x_p, x_s, mod, w_out_b, post_g, ffn_g, wr_hi, wr_lo, rbias)


def _stage_indices(idx_vmem, idx_smem, sem):
    cp = pltpu.make_async_copy(idx_vmem.at[0, 0], idx_smem, sem)
    cp.start()
    cp.wait()


def _issue_row_dmas(idx_smem, copy_of):
    def slot_body(j, carry):
        def chunk(c, carry):
            for u in range(ISSUE_UNROLL):
                tok = c * ISSUE_UNROLL + u
                copy_of(j, tok, pl.multiple_of(idx_smem[j * TM + tok], CHUNKS)).start(
                    priority=u % DMA_THREADS)
            return carry
        return lax.fori_loop(0, TM // ISSUE_UNROLL, chunk, carry)
    lax.fori_loop(0, TOP_K, slot_body, 0)


def _dispatch_kernel(idx_ref, h_ref, xs_hbm, idx_smem, isem, dsem):
    _stage_indices(idx_ref, idx_smem, isem)

    def copy_of(j, tok, dst):
        src = pl.multiple_of(tok * CHUNKS, CHUNKS)
        return pltpu.make_async_copy(h_ref.at[pl.ds(src, CHUNKS)], xs_hbm.at[pl.ds(dst, CHUNKS)], dsem)

    _issue_row_dmas(idx_smem, copy_of)
    for _ in range(TOP_K):
        pltpu.make_async_copy(h_ref, xs_hbm.at[pl.ds(0, TM * CHUNKS)], dsem).wait()


def _dispatch(dest_tiles, h_packed):
    n_tiles = dest_tiles.shape[0]
    rows = n_tiles * TM * TOP_K
    return pl.pallas_call(
        _dispatch_kernel,
        grid=(n_tiles,),
        in_specs=[pl.BlockSpec((1, 1, TOP_K * TM), lambda i: (i, 0, 0)),
                  pl.BlockSpec((TM * CHUNKS, LANES), lambda i: (i, 0))],
        out_specs=pl.BlockSpec(memory_space=pl.ANY),
        out_shape=jax.ShapeDtypeStruct((rows * CHUNKS, LANES), U32),
        scratch_shapes=[pltpu.SMEM((TOP_K * TM,), I32),
                        pltpu.SemaphoreType.DMA,
                        pltpu.SemaphoreType.DMA],
        compiler_params=_cparams(("arbitrary",)),
        name="dispatch",
    )(dest_tiles, h_packed)


def _experts_kernel(vb_ref, ve_ref, vlo_ref, vhi_ref, vfirst_ref, vseq_ref, nvis_ref,
                    x_ref, wg_ref, wu_ref, wd_ref, out_ref, wgb, wub, wdb, *, n_visits):
    v = pl.program_id(0)
    nxt = jnp.minimum(v + 1, n_visits - 1)

    @pl.when((v == 0) | (ve_ref[nxt] != ve_ref[v]))
    def _():
        s = vseq_ref[nxt] % 2
        wgb[s] = wg_ref[0].astype(BF16)
        wub[s] = wu_ref[0].astype(BF16)

    @pl.when((v == 0) | (ve_ref[v] != ve_ref[jnp.maximum(v - 1, 0)]))
    def _():
        wdb[...] = wd_ref[0].astype(BF16)

    @pl.when((v >= 1) & (v <= nvis_ref[0]))
    def _():
        @pl.when(vfirst_ref[v] == 1)
        def _():
            out_ref[...] = jnp.zeros_like(out_ref)

        s = vseq_ref[v] % 2
        y = _swiglu_packed(_load_rows(x_ref, TE), wgb.at[s], wub.at[s], wdb)
        rows = lax.broadcasted_iota(I32, (TE, 1), 0)
        mine = (rows >= vlo_ref[v]) & (rows < vhi_ref[v])
        y = jnp.where(mine, _pack_bf16_pair(y[:, :HALF], y[:, HALF:]), _load_rows(out_ref, TE))
        _store_rows(out_ref, TE, y)


def _experts(visits, xs, w_gate_e, w_up_e, w_down_e):
    n_visits = visits[0].shape[0]
    n_rows = xs.shape[0] // CHUNKS
    ahead = lambda v: jnp.minimum(v + 1, n_visits - 1)
    wspec = lambda shape, at: pl.BlockSpec((1,) + shape, lambda v, vb, ve, *_: (ve[at(v)], 0, 0))
    rows = pl.BlockSpec((TE * CHUNKS, LANES), lambda v, vb, *_: (vb[v], 0))
    grid_spec = pltpu.PrefetchScalarGridSpec(
        num_scalar_prefetch=7,
        grid=(n_visits,),
        in_specs=[rows, wspec((D_MODEL, D_EXPERT), ahead), wspec((D_MODEL, D_EXPERT), ahead),
                  wspec((D_EXPERT, D_MODEL), lambda v: v)],
        out_specs=rows,
        scratch_shapes=[pltpu.VMEM((2, D_MODEL, D_EXPERT), BF16),
                        pltpu.VMEM((2, D_MODEL, D_EXPERT), BF16),
                        pltpu.VMEM((D_EXPERT, D_MODEL), BF16)])
    return pl.pallas_call(
        functools.partial(_experts_kernel, n_visits=n_visits),
        grid_spec=grid_spec,
        out_shape=jax.ShapeDtypeStruct((n_rows * CHUNKS, LANES), U32),
        compiler_params=_cparams(("arbitrary",)),
        name="experts",
    )(*visits, xs, w_gate_e, w_up_e, w_down_e)


def _ffn_out_kernel(cur_ref, nxt_ref, yb_hbm, gates_ref, h_ref, y1_ref, mod_ref, fg_ref,
                    wg_ref, wu_ref, wd_ref, out_ref, idx_smem, gbuf, ffn_ref, isem, gsem,
                    *, n_tiles, tiles_per_row, mod_row0):
    i = pl.program_id(0)
    slot = i % 2
    n_idx = TOP_K * TM

    def gather(idx_vmem, s):
        _stage_indices(idx_vmem, idx_smem, isem)
        _issue_row_dmas(idx_smem, lambda j, tok, src: pltpu.make_async_copy(
            yb_hbm.at[pl.ds(src, CHUNKS)],
            gbuf.at[s, pl.ds(pl.multiple_of((j * TM + tok) * CHUNKS, CHUNKS), CHUNKS)], gsem.at[s]))

    @pl.when(i == 0)
    def _():
        gather(cur_ref, 0)

    @pl.when(i + 1 < n_tiles)
    def _():
        gather(nxt_ref, 1 - slot)

    shared = _swiglu_packed(_load_rows(h_ref, TM), wg_ref, wu_ref, wd_ref)

    pltpu.make_async_copy(yb_hbm.at[pl.ds(0, n_idx * CHUNKS)], gbuf.at[slot], gsem.at[slot]).wait()
    g = gates_ref[...]
    gate = [jnp.broadcast_to(g[:, j:j + 1], (TM, LANES)) for j in range(TOP_K)]
    rows = gbuf.at[slot]
    for c in range(CHUNKS):
        acc_lo = shared[:, c * LANES:(c + 1) * LANES]
        acc_hi = shared[:, HALF + c * LANES:HALF + (c + 1) * LANES]
        for j in range(TOP_K):
            r_lo, r_hi = _unpack_bf16_pair(rows[pl.ds(j * TM * CHUNKS + c, TM, stride=CHUNKS), :])
            acc_lo = acc_lo + gate[j] * r_lo
            acc_hi = acc_hi + gate[j] * r_hi
        ffn_ref[:, c * LANES:(c + 1) * LANES] = acc_lo
        ffn_ref[:, HALF + c * LANES:HALF + (c + 1) * LANES] = acc_hi
    row = mod_row0 + (i // tiles_per_row if tiles_per_row else 0)
    out_ref[...] = y1_ref[...] + _mod_chunk(mod_ref, row, 5) * _rms(ffn_ref[...], fg_ref[...])


def _ffn_out(dest_tiles, yb, gates_t, h_packed, y1, mod, post_g, wg_b, wu_b, wd_b,
             *, tile0, n_tiles, tiles_per_row, mod_row0):
    full = lambda shape: pl.BlockSpec(shape, lambda i: (0,) * len(shape))
    tok = lambda w: pl.BlockSpec((TM, w), lambda i: (tile0 + i, 0))
    last = tile0 + n_tiles - 1
    idx = lambda step: pl.BlockSpec((1, 1, TOP_K * TM),
                                    lambda i: (jnp.minimum(tile0 + i + step, last), 0, 0))
    return pl.pallas_call(
        functools.partial(_ffn_out_kernel, n_tiles=n_tiles,
                          tiles_per_row=tiles_per_row, mod_row0=mod_row0),
        grid=(n_tiles,),
        in_specs=[idx(0), idx(1), pl.BlockSpec(memory_space=pl.ANY),
                  tok(TOP_K), pl.BlockSpec((TM * CHUNKS, LANES), lambda i: (tile0 + i, 0)),
                  tok(D_MODEL), full(mod.shape), full((1, D_MODEL)),
                  full((D_MODEL, D_EXPERT)), full((D_MODEL, D_EXPERT)), full((D_EXPERT, D_MODEL))],
        out_specs=pl.BlockSpec((TM, D_MODEL), lambda i: (i, 0)),
        out_shape=jax.ShapeDtypeStruct((n_tiles * TM, D_MODEL), F32),
        scratch_shapes=[pltpu.SMEM((TOP_K * TM,), I32),
                        pltpu.VMEM((2, TOP_K * TM * CHUNKS, LANES), U32),
                        pltpu.VMEM((TM, D_MODEL), F32),
                        pltpu.SemaphoreType.DMA,
                        pltpu.SemaphoreType.DMA((2,))],
        compiler_params=_cparams(("arbitrary",)),
        name="ffn_out",
    )(dest_tiles, dest_tiles, yb, gates_t, h_packed, y1, mod, post_g, wg_b, wu_b, wd_b)


def _sorted_rows_kernel(ids_ref, dest_ref, counts_ref, carry):
    p, i = pl.program_id(0), pl.program_id(1)
    tm = ids_ref.shape[1]
    ids = ids_ref[...]
    expert = lax.broadcasted_iota(I32, (N_EXPERTS, tm), 0)
    chose = [ids[j:j + 1, :] == expert for j in range(TOP_K)]
    onehot = sum(jnp.where(c, 1.0, 0.0) for c in chose)
    tile_counts = jnp.broadcast_to(onehot.sum(axis=1, keepdims=True), (N_EXPERTS, LANES))

    @pl.when((p == 0) & (i == 0))
    def _():
        carry[...] = jnp.zeros_like(carry)

    @pl.when(p == 0)
    def _():
        carry[...] += tile_counts

    @pl.when((p == 1) & (i == 0))
    def _():
        counts = carry[...]
        counts_ref[...] = counts
        row = lax.broadcasted_iota(I32, (N_EXPERTS, LANES), 0)
        incl = counts
        shift = 1
        while shift < N_EXPERTS:
            incl = incl + jnp.where(row >= shift, pltpu.roll(incl, shift, 0), 0.0)
            shift *= 2
        carry[...] = incl - counts

    @pl.when(p == 1)
    def _():
        earlier = (lax.broadcasted_iota(I32, (tm, tm), 0) < lax.broadcasted_iota(I32, (tm, tm), 1))
        before = jnp.dot(onehot.astype(BF16), jnp.where(earlier, 1.0, 0.0).astype(BF16),
                         preferred_element_type=F32)
        pos = carry[:, 0:1] + before
        rows = [jnp.where(c, pos, 0.0).sum(axis=0, keepdims=True) for c in chose]
        dest_ref[...] = jnp.concatenate(rows, axis=0).astype(I32)
        carry[...] += tile_counts


def _sorted_rows(ids):
    t = ids.shape[1]
    n_tiles = t // TM
    return pl.pallas_call(
        _sorted_rows_kernel,
        grid=(2, n_tiles),
        in_specs=[pl.BlockSpec((TOP_K, TM), lambda p, i: (0, i))],
        out_specs=[pl.BlockSpec((TOP_K, TM), lambda p, i: (0, i * p)),
                   pl.BlockSpec((N_EXPERTS, LANES), lambda p, i: (0, 0))],
        out_shape=[jax.ShapeDtypeStruct((TOP_K, t), I32),
                   jax.ShapeDtypeStruct((N_EXPERTS, LANES), F32)],
        scratch_shapes=[pltpu.VMEM((N_EXPERTS, LANES), F32)],
        compiler_params=_cparams(("arbitrary", "arbitrary")),
        name="sorted_rows",
    )(ids)


def _routing_tables(ids):
    t = ids.shape[1]
    n_blocks = t * TOP_K // TE
    n_visits = n_blocks + N_EXPERTS - 1
    dest, counts = _sorted_rows(ids)
    counts = counts[:, 0].astype(I32)
    end = jnp.cumsum(counts)
    start = end - counts

    first_blk = start // TE
    n_vis_e = jnp.where(counts > 0, (end - 1) // TE - first_blk + 1, 0)
    vend = jnp.cumsum(n_vis_e)
    vstart = vend - n_vis_e
    n_vis = vend[-1]
    v = jnp.arange(n_visits, dtype=I32)
    valid = v < n_vis
    mine = (vstart[None, :] <= v[:, None]) & (v[:, None] < vend[None, :])
    pick = lambda x: jnp.where(mine, x[None, :], 0).sum(axis=1)
    experts = jnp.arange(N_EXPERTS, dtype=I32)
    last_expert = jnp.max(jnp.where(n_vis_e > 0, experts, 0))
    ve = jnp.where(valid, pick(experts), last_expert).astype(I32)
    vb = jnp.where(valid, pick(first_blk) + v - pick(vstart), n_blocks - 1).astype(I32)
    lo = jnp.where(valid, jnp.clip(pick(start) - vb * TE, 0, TE), 0).astype(I32)
    hi = jnp.where(valid, jnp.clip(pick(end) - vb * TE, 0, TE), 0).astype(I32)
    vfirst = (valid & ((v == 0) | (vb != jnp.roll(vb, 1)))).astype(I32)
    lead = lambda x, first: jnp.concatenate([first, x])
    zero = jnp.zeros((1,), I32)
    vb, ve = lead(vb, vb[:1]), lead(ve, ve[:1])
    lo, hi, vfirst = lead(lo, zero), lead(hi, zero), lead(vfirst, zero)
    changed = (ve != jnp.roll(ve, 1)) & (jnp.arange(ve.shape[0]) > 0)
    vseq = jnp.cumsum(changed.astype(I32))
    return dest.astype(I32), (vb, ve, lo, hi, vfirst, vseq.astype(I32), n_vis.reshape(1).astype(I32))


def _rope_tables(n_tokens):
    rows = n_tokens // GRID_W
    row = jnp.repeat(jnp.arange(rows, dtype=F32), GRID_W)
    col = jnp.tile(jnp.arange(GRID_W, dtype=F32), rows)
    inv_freq = ROPE_THETA ** (-jnp.arange(0, AXIS_DIM, 2, dtype=F32) / AXIS_DIM)
    ang_r = row[:, None] * inv_freq
    ang_c = col[:, None] * inv_freq
    ang = jnp.concatenate([ang_r, ang_r, ang_c, ang_c], axis=-1)
    cos, sin = jnp.cos(ang), jnp.sin(ang)
    first = (jnp.arange(HEAD_DIM) % AXIS_DIM) < (AXIS_DIM // 2)
    return cos, jnp.where(first, -sin, 0.0), jnp.where(first, 0.0, sin)


def kernel(x_prompt, x_sample, cache_glob_k, cache_glob_v, cache_win_k, cache_win_v, c, c_ctx,
           w_ada, b_ada, attn_pre_g, attn_post_g, w_in, q_norm_g, k_norm_g, sink_logit, w_out,
           ffn_pre_g, ffn_post_g, w_router, router_bias, w_gate_e, w_up_e, w_down_e,
           w_gate_s, w_up_s, w_down_s):
    batch, seq, d = x_prompt.shape
    dec_batch, dec_seq, _ = x_sample.shape
    past = cache_glob_k.shape[2]
    tp, ts = batch * seq, dec_batch * dec_seq
    t = tp + ts
    l = 0

    xp = x_prompt.reshape(tp, d)
    xs = x_sample.reshape(ts, d)
    cond = jnp.concatenate([c_ctx[None, :], c, jnp.zeros((MOD_ROWS - 1 - dec_batch, d), F32)], axis=0)
    mod = _adaln(cond, w_ada[l], b_ada[l][None, :])

    w_in_b = w_in[l].astype(BF16)
    pre_g = attn_pre_g[l][None, :]
    qn, kn = q_norm_g[l][None, :], k_norm_g[l][None, :]
    cos, sa, sb = _rope_tables(dec_seq)
    sink = sink_logit[l]

    qg, kg, vg, qw, kw, vw = _qkv(xp, mod, pre_g, w_in_b, qn, kn, cos, sa, sb,
                                  rope=False, seq=seq, kv_dtype=F32)
    o_p = _ctx_attention(sink, qg, kg, vg, qw, kw, vw, seq=seq)
    new_kv = [a.reshape(batch, 1, seq, N_KV, HEAD_DIM) for a in (kg, vg, kw, vw)]

    qg, kg, vg, qw, kw, vw = _qkv(xs, mod, pre_g, w_in_b, qn, kn, cos, sa, sb,
                                  rope=True, seq=dec_seq, kv_dtype=BF16)
    caches = [a[:, l].reshape(dec_batch * past, KV_W)
              for a in (cache_glob_k, cache_glob_v, cache_win_k, cache_win_v)]
    o_s = _lat_attention(sink, qg, kg, vg, qw, kw, vw, *caches, seq=dec_seq, past=past)

    wr_t = w_router[l].T
    wr_hi = wr_t.astype(BF16)
    wr_lo = (wr_t - wr_hi.astype(F32)).astype(BF16)
    y1, h_packed, ids, gates = _post_attn(
        o_p, o_s, xp, xs, mod, w_out[l].astype(BF16), attn_post_g[l][None, :], ffn_pre_g[l][None, :],
        wr_hi, wr_lo, router_bias[l][:, None], seq=dec_seq)

    dest, visits = _routing_tables(ids)
    dest_tiles = (dest * CHUNKS).reshape(TOP_K, t // TM, TM).transpose(1, 0, 2).reshape(t // TM, 1, TOP_K * TM)
    xs_rows = _dispatch(dest_tiles, h_packed)
    yb = _experts(visits, xs_rows, w_gate_e[l], w_up_e[l], w_down_e[l])

    gates_t = gates.T
    shared = (w_gate_s[l].astype(BF16), w_up_s[l].astype(BF16), w_down_s[l].astype(BF16))
    post_g = ffn_post_g[l][None, :]
    y_p = _ffn_out(dest_tiles, yb, gates_t, h_packed, y1, mod, post_g, *shared,
                   tile0=0, n_tiles=tp // TM, tiles_per_row=0, mod_row0=0)
    y_s = _ffn_out(dest_tiles, yb, gates_t, h_packed, y1, mod, post_g, *shared,
                   tile0=tp // TM, n_tiles=ts // TM, tiles_per_row=dec_seq // TM, mod_row0=1)
    return (y_p.reshape(batch, seq, d), y_s.reshape(dec_batch, dec_seq, d), *new_kv)
```

```python
import functools

import jax
import jax.numpy as jnp
from jax import lax
from jax.experimental import pallas as pl
from jax.experimental.pallas import tpu as pltpu

F32 = jnp.float32
BF16 = jnp.bfloat16
I32 = jnp.int32

D_MODEL = 2048
GRID_W = 64
HEAD_DIM = 128
AXIS_DIM = HEAD_DIM // 2
N_HEADS = 8
N_KV = 2
GROUP = N_HEADS // N_KV
Q_W = N_HEADS * HEAD_DIM
KV_W = N_KV * HEAD_DIM
QKV_WIDTH = 2 * Q_W + 4 * KV_W
WINDOW = 128
ROPE_THETA = 10000.0
NORM_EPS = 1e-6
N_EXPERTS = 64
N_GROUPS = 8
TOPK_GROUPS = 4
TOP_K = 8
D_EXPERT = 512
ROUTED_SCALE = 2.5

LANES = 128
CHUNKS = D_MODEL // LANES
MOD_ROWS = 8
VMEM_LIMIT = 56 * 1024 * 1024

TM = 256
TD = 128
TQ = 256
TE = 256
NEG = -1e30
ISSUE_UNROLL = 32
DMA_THREADS = 2


def _cparams(sem):
    return pltpu.CompilerParams(dimension_semantics=sem, vmem_limit_bytes=VMEM_LIMIT)


def _store_rows(ref, n, x):
    for c in range(CHUNKS):
        ref[pl.ds(c, n, stride=CHUNKS), :] = x[:, c * LANES:(c + 1) * LANES]


def _load_rows(ref, n, first=0):
    return jnp.concatenate(
        [ref[pl.ds(first * CHUNKS + c, n, stride=CHUNKS), :] for c in range(CHUNKS)], axis=-1)


def _rms(x, gain):
    r = lax.rsqrt(jnp.mean(x * x, axis=-1, keepdims=True) + NORM_EPS)
    return x * r * gain


def _dot_nt(a, b):
    return lax.dot_general(a, b, (((1,), (1,)), ((), ())), preferred_element_type=F32)


def _swiglu(x, wg, wu, wd):
    xb = x.astype(BF16)
    gate = jnp.dot(xb, wg[...], preferred_element_type=F32)
    up = jnp.dot(xb, wu[...], preferred_element_type=F32)
    act = (gate / (1.0 + jnp.exp(-gate)) * up).astype(BF16)
    return jnp.dot(act, wd[...], preferred_element_type=F32)


def _adaln_kernel(cond_ref, w_ref, b_ref, o_ref):
    c = cond_ref[...]
    s = (c / (1.0 + jnp.exp(-c))).astype(BF16)
    o_ref[...] = jnp.dot(s, w_ref[...].astype(BF16), preferred_element_type=F32) + b_ref[...]


def _adaln(cond, w_ada, b_ada):
    n = w_ada.shape[1]
    tn = 1024
    return pl.pallas_call(
        _adaln_kernel,
        grid=(n // tn,),
        in_specs=[pl.BlockSpec((MOD_ROWS, D_MODEL), lambda j: (0, 0)),
                  pl.BlockSpec((D_MODEL, tn), lambda j: (0, j)),
                  pl.BlockSpec((1, tn), lambda j: (0, j))],
        out_specs=pl.BlockSpec((MOD_ROWS, tn), lambda j: (0, j)),
        out_shape=jax.ShapeDtypeStruct((MOD_ROWS, n), F32),
        compiler_params=_cparams(("arbitrary",)),
        name="adaln",
    )(cond, w_ada, b_ada)


def _mod_chunk(mod_ref, row, k):
    return mod_ref[pl.ds(row, 1), k * D_MODEL:(k + 1) * D_MODEL]


def _qkv_kernel(x_ref, mod_ref, g_ref, w_ref, qn_ref, kn_ref, cos_ref, sa_ref, sb_ref,
                qg_ref, kg_ref, vg_ref, qw_ref, kw_ref, vw_ref, *, rope, tiles_per_row):
    i = pl.program_id(0)
    row = (1 + i // tiles_per_row) if rope else 0
    x = x_ref[...]
    h = _rms(x, g_ref[...]) * (1.0 + _mod_chunk(mod_ref, row, 1)) + _mod_chunk(mod_ref, row, 0)
    proj = jnp.dot(h.astype(BF16), w_ref[...], preferred_element_type=F32)

    if rope:
        cos, sa, sb = cos_ref[...], sa_ref[...], sb_ref[...]

    def rot(t):
        if not rope:
            return t
        return t * cos + pltpu.roll(t, 96, 1) * sa + pltpu.roll(t, 32, 1) * sb

    scale = HEAD_DIM ** -0.5
    qn, kn = qn_ref[...], kn_ref[...]
    off = 0
    for hd in range(N_HEADS):
        t = proj[:, off + hd * HEAD_DIM: off + (hd + 1) * HEAD_DIM]
        qg_ref[:, hd * HEAD_DIM:(hd + 1) * HEAD_DIM] = (rot(_rms(t, qn)) * scale).astype(qg_ref.dtype)
    off += Q_W
    for hd in range(N_KV):
        t = proj[:, off + hd * HEAD_DIM: off + (hd + 1) * HEAD_DIM]
        kg_ref[:, hd * HEAD_DIM:(hd + 1) * HEAD_DIM] = rot(_rms(t, kn)).astype(kg_ref.dtype)
    off += KV_W
    vg_ref[...] = proj[:, off:off + KV_W].astype(vg_ref.dtype)
    off += KV_W
    for hd in range(N_HEADS):
        t = proj[:, off + hd * HEAD_DIM: off + (hd + 1) * HEAD_DIM]
        qw_ref[:, hd * HEAD_DIM:(hd + 1) * HEAD_DIM] = (rot(t) * scale).astype(qw_ref.dtype)
    off += Q_W
    for hd in range(N_KV):
        t = proj[:, off + hd * HEAD_DIM: off + (hd + 1) * HEAD_DIM]
        kw_ref[:, hd * HEAD_DIM:(hd + 1) * HEAD_DIM] = rot(t).astype(kw_ref.dtype)
    off += KV_W
    vw_ref[...] = proj[:, off:off + KV_W].astype(vw_ref.dtype)


def _qkv(x2d, mod, pre_g, w_in_b, qn, kn, cos, sa, sb, *, rope, seq, kv_dtype):
    t = x2d.shape[0]
    tiles_per_row = seq // TM
    full = lambda shape: pl.BlockSpec(shape, lambda i: (0,) * len(shape))
    tab = pl.BlockSpec((TM, HEAD_DIM), (lambda i: (i % tiles_per_row, 0)) if rope else (lambda i: (0, 0)))
    tok = lambda w: pl.BlockSpec((TM, w), lambda i: (i, 0))
    return pl.pallas_call(
        functools.partial(_qkv_kernel, rope=rope, tiles_per_row=tiles_per_row),
        grid=(t // TM,),
        in_specs=[tok(D_MODEL), full(mod.shape), full((1, D_MODEL)),
                  full((D_MODEL, QKV_WIDTH)), full((1, HEAD_DIM)), full((1, HEAD_DIM)),
                  tab, tab, tab],
        out_specs=[tok(Q_W), tok(KV_W), tok(KV_W), tok(Q_W), tok(KV_W), tok(KV_W)],
        out_shape=[jax.ShapeDtypeStruct((t, Q_W), BF16),
                   jax.ShapeDtypeStruct((t, KV_W), kv_dtype),
                   jax.ShapeDtypeStruct((t, KV_W), kv_dtype),
                   jax.ShapeDtypeStruct((t, Q_W), BF16),
                   jax.ShapeDtypeStruct((t, KV_W), kv_dtype),
                   jax.ShapeDtypeStruct((t, KV_W), kv_dtype)],
        compiler_params=_cparams(("arbitrary",)),
        name="qkv_rope" if rope else "qkv_ctx",
    )(x2d, mod, pre_g, w_in_b, qn, kn, cos, sa, sb)


def _softmax_pv(scores, values, sink):
    m = scores[0].max(axis=-1, keepdims=True)
    for s in scores[1:]:
        m = jnp.maximum(m, s.max(axis=-1, keepdims=True))
    if sink is not None:
        m = jnp.maximum(m, sink)
    den = jnp.exp(sink - m) if sink is not None else 0.0
    acc = None
    for s, v in zip(scores, values):
        p = jnp.exp(s - m)
        den = den + p.sum(axis=-1, keepdims=True)
        pv = jnp.dot(p.astype(BF16), v, preferred_element_type=F32)
        acc = pv if acc is None else acc + pv
    return acc / den


def _ctx_attn_kernel(sink_ref, qg_ref, kg_ref, vg_ref, qw_ref, kw_ref, vw_ref, o_ref):
    for mixer, (q_ref, k_ref, v_ref) in enumerate(((qg_ref, kg_ref, vg_ref), (qw_ref, kw_ref, vw_ref))):
        for n in range(N_KV):
            k = k_ref[:, n * HEAD_DIM:(n + 1) * HEAD_DIM].astype(BF16)
            v = v_ref[:, n * HEAD_DIM:(n + 1) * HEAD_DIM].astype(BF16)
            for g in range(GROUP):
                hd = n * GROUP + g
                q = q_ref[:, hd * HEAD_DIM:(hd + 1) * HEAD_DIM]
                sink = sink_ref[hd] if mixer == 1 else None
                o = _softmax_pv([_dot_nt(q, k)], [v], sink)
                c0 = mixer * Q_W + hd * HEAD_DIM
                o_ref[:, c0:c0 + HEAD_DIM] = o.astype(o_ref.dtype)


def _ctx_attention(sink, qg, kg, vg, qw, kw, vw, *, seq):
    t = qg.shape[0]
    blk = lambda w: pl.BlockSpec((seq, w), lambda b: (b, 0))
    return pl.pallas_call(
        _ctx_attn_kernel,
        grid=(t // seq,),
        in_specs=[pl.BlockSpec(memory_space=pltpu.SMEM),
                  blk(Q_W), blk(KV_W), blk(KV_W), blk(Q_W), blk(KV_W), blk(KV_W)],
        out_specs=blk(2 * Q_W),
        out_shape=jax.ShapeDtypeStruct((t, 2 * Q_W), BF16),
        compiler_params=_cparams(("arbitrary",)),
        name="attn_ctx",
    )(sink, qg, kg, vg, qw, kw, vw)


def _lat_attn_kernel(sink_ref, qg_ref, kg_ref, vg_ref, qw_ref, kw_ref, vw_ref,
                     cgk_ref, cgv_ref, cwk_ref, cwv_ref, o_ref, *, seq):
    qi = pl.program_id(1)
    span = TQ + 2 * WINDOW
    start = pl.multiple_of(jnp.clip(qi * TQ - WINDOW, 0, seq - span), WINDOW)
    qpos = qi * TQ + lax.broadcasted_iota(I32, (TQ, span), 0)
    kpos = start + lax.broadcasted_iota(I32, (TQ, span), 1)
    band_ok = jnp.abs(kpos - qpos) <= WINDOW
    for n in range(N_KV):
        cols = slice(n * HEAD_DIM, (n + 1) * HEAD_DIM)
        kc, vc = cgk_ref[:, cols].astype(BF16), cgv_ref[:, cols].astype(BF16)
        kl, vl = kg_ref[:, cols], vg_ref[:, cols]
        for g in range(GROUP):
            hd = n * GROUP + g
            q = qg_ref[:, hd * HEAD_DIM:(hd + 1) * HEAD_DIM]
            o = _softmax_pv([_dot_nt(q, kc), _dot_nt(q, kl)], [vc, vl], None)
            o_ref[:, hd * HEAD_DIM:(hd + 1) * HEAD_DIM] = o.astype(o_ref.dtype)
        kc, vc = cwk_ref[:, cols].astype(BF16), cwv_ref[:, cols].astype(BF16)
        kl, vl = kw_ref[pl.ds(start, span), cols], vw_ref[pl.ds(start, span), cols]
        for g in range(GROUP):
            hd = n * GROUP + g
            q = qw_ref[:, hd * HEAD_DIM:(hd + 1) * HEAD_DIM]
            band = jnp.where(band_ok, _dot_nt(q, kl), NEG)
            o = _softmax_pv([_dot_nt(q, kc), band], [vc, vl], sink_ref[hd])
            c0 = Q_W + hd * HEAD_DIM
            o_ref[:, c0:c0 + HEAD_DIM] = o.astype(o_ref.dtype)


def _lat_attention(sink, qg, kg, vg, qw, kw, vw, cgk, cgv, cwk, cwv, *, seq, past):
    t = qg.shape[0]
    nq = seq // TQ
    qblk = pl.BlockSpec((TQ, Q_W), lambda b, i: (b * nq + i, 0))
    kblk = pl.BlockSpec((seq, KV_W), lambda b, i: (b, 0))
    cblk = pl.BlockSpec((past, KV_W), lambda b, i: (b, 0))
    return pl.pallas_call(
        functools.partial(_lat_attn_kernel, seq=seq),
        grid=(t // seq, nq),
        in_specs=[pl.BlockSpec(memory_space=pltpu.SMEM),
                  qblk, kblk, kblk, qblk, kblk, kblk, cblk, cblk, cblk, cblk],
        out_specs=pl.BlockSpec((TQ, 2 * Q_W), lambda b, i: (b * nq + i, 0)),
        out_shape=jax.ShapeDtypeStruct((t, 2 * Q_W), BF16),
        compiler_params=_cparams(("arbitrary", "arbitrary")),
        name="attn_lat",
    )(sink, qg, kg, vg, qw, kw, vw, cgk, cgv, cwk, cwv)


def _route(sel, scores, tm):
    per = N_EXPERTS // N_GROUPS
    sel3 = sel.reshape(N_GROUPS, per, tm)
    sc3 = scores.reshape(N_GROUPS, per, tm)
    member = lax.broadcasted_iota(I32, (N_GROUPS, per, tm), 1)
    m1 = sel3.max(axis=1, keepdims=True)
    first = jnp.where(sel3 == m1, member, per).min(axis=1, keepdims=True)
    m2 = jnp.where(member == first, -jnp.inf, sel3).max(axis=1, keepdims=True)
    gs = m1 + m2

    gid = lax.broadcasted_iota(I32, (N_GROUPS, 1, tm), 0)
    chosen = jnp.zeros((N_GROUPS, 1, tm), F32)
    for _ in range(TOPK_GROUPS):
        gm = gs.max(axis=0, keepdims=True)
        gfirst = jnp.where(gs == gm, gid, N_GROUPS).min(axis=0, keepdims=True)
        hit = gid == gfirst
        chosen = jnp.where(hit, 1.0, chosen)
        gs = jnp.where(hit, -jnp.inf, gs)

    eid = lax.broadcasted_iota(I32, (N_GROUPS, per, tm), 0) * per + member
    masked = jnp.where(chosen > 0.0, sel3, -jnp.inf)
    ids, raw = [], []
    for _ in range(TOP_K):
        mx = masked.max(axis=1, keepdims=True).max(axis=0, keepdims=True)
        efirst = jnp.where(masked == mx, eid, N_EXPERTS).min(axis=1, keepdims=True).min(axis=0, keepdims=True)
        hit = eid == efirst
        ids.append(efirst.reshape(1, tm))
        raw.append(jnp.where(hit, sc3, 0.0).sum(axis=1, keepdims=True).sum(axis=0, keepdims=True).reshape(1, tm))
        masked = jnp.where(hit, -jnp.inf, masked)
    ids = jnp.concatenate(ids, axis=0)
    raw = jnp.concatenate(raw, axis=0)
    gates = raw / raw.sum(axis=0, keepdims=True) * ROUTED_SCALE
    return ids, gates


def _post_attn_kernel(op_ref, os_ref, xp_ref, xs_ref, mod_ref, w_ref, pg_ref, fg_ref,
                      wrh_ref, wrl_ref, rb_ref,
                      y_ref, h_ref, ids_ref, gates_ref, *, ctx_tiles, tiles_per_row):
    i = pl.program_id(0)
    is_ctx = i < ctx_tiles
    row = jnp.where(is_ctx, 0, 1 + (i - ctx_tiles) // tiles_per_row)
    o = jnp.where(is_ctx, op_ref[...], os_ref[...])
    x = jnp.where(is_ctx, xp_ref[...], xs_ref[...])
    a = jnp.dot(o, w_ref[...], preferred_element_type=F32)
    y = x + _mod_chunk(mod_ref, row, 2) * _rms(a, pg_ref[...])
    y_ref[...] = y
    h = _rms(y, fg_ref[...]) * (1.0 + _mod_chunk(mod_ref, row, 4)) + _mod_chunk(mod_ref, row, 3)
    _store_rows(h_ref, h.shape[0], h)
    h_hi = h.astype(BF16)
    h_lo = (h - h_hi.astype(F32)).astype(BF16)
    wh, wl = wrh_ref[...], wrl_ref[...]
    logits = _dot_nt(wh, h_hi) + (_dot_nt(wl, h_hi) + _dot_nt(wh, h_lo))
    scores = 1.0 / (1.0 + jnp.exp(-logits))
    ids, gates = _route(scores + rb_ref[...], scores, logits.shape[1])
    ids_ref[...] = ids
    gates_ref[...] = gates


def _post_attn(o_p, o_s, x_p, x_s, mod, w_out_b, post_g, ffn_g, wr_hi, wr_lo, rbias, *, seq):
    tp, ts = x_p.shape[0], x_s.shape[0]
    t = tp + ts
    ctx_tiles = tp // TM
    full = lambda shape: pl.BlockSpec(shape, lambda i: (0,) * len(shape))
    ctx = lambda w: pl.BlockSpec((TM, w), lambda i: (jnp.minimum(i, ctx_tiles - 1), 0))
    lat = lambda w: pl.BlockSpec((TM, w), lambda i: (jnp.maximum(i - ctx_tiles, 0), 0))
    tok = lambda w: pl.BlockSpec((TM, w), lambda i: (i, 0))
    lane = pl.BlockSpec((TOP_K, TM), lambda i: (0, i))
    return pl.pallas_call(
        functools.partial(_post_attn_kernel, ctx_tiles=ctx_tiles, tiles_per_row=seq // TM),
        grid=(t // TM,),
        in_specs=[ctx(D_MODEL), lat(D_MODEL), ctx(D_MODEL), lat(D_MODEL), full(mod.shape),
                  full((D_MODEL, D_MODEL)), full((1, D_MODEL)), full((1, D_MODEL)),
                  full((N_EXPERTS, D_MODEL)), full((N_EXPERTS, D_MODEL)), full((N_EXPERTS, 1))],
        out_specs=[tok(D_MODEL), pl.BlockSpec((TM * CHUNKS, LANES), lambda i: (i, 0)), lane, lane],
        out_shape=[jax.ShapeDtypeStruct((t, D_MODEL), F32),
                   jax.ShapeDtypeStruct((t * CHUNKS, LANES), F32),
                   jax.ShapeDtypeStruct((TOP_K, t), I32),
                   jax.ShapeDtypeStruct((TOP_K, t), F32)],
        compiler_params=_cparams(("arbitrary",)),
        name="post_attn_router",
    )(o_p, o_s, x_p, x_s, mod, w_out_b, post_g, ffn_g, wr_hi, wr_lo, rbias)


def _stage_indices(idx_vmem, idx_smem, sem):
    cp = pltpu.make_async_copy(idx_vmem.at[0, 0], idx_smem, sem)
    cp.start()
    cp.wait()


def _issue_row_dmas(idx_smem, copy_of):
    def slot_body(j, carry):
        def chunk(c, carry):
            for u in range(ISSUE_UNROLL):
                tok = c * ISSUE_UNROLL + u
                copy_of(j, tok, pl.multiple_of(idx_smem[j * TD + tok], CHUNKS)).start(
                    priority=u % DMA_THREADS)
            return carry
        return lax.fori_loop(0, TD // ISSUE_UNROLL, chunk, carry)
    lax.fori_loop(0, TOP_K, slot_body, 0)


def _dispatch_kernel(idx_ref, h_ref, xs_hbm, idx_smem, isem, dsem):
    _stage_indices(idx_ref, idx_smem, isem)

    def copy_of(j, tok, dst):
        src = pl.multiple_of(tok * CHUNKS, CHUNKS)
        return pltpu.make_async_copy(h_ref.at[pl.ds(src, CHUNKS)], xs_hbm.at[pl.ds(dst, CHUNKS)], dsem)

    _issue_row_dmas(idx_smem, copy_of)
    for _ in range(TOP_K):
        pltpu.make_async_copy(h_ref, xs_hbm.at[pl.ds(0, TD * CHUNKS)], dsem).wait()


def _dispatch(dest_tiles, h_rows):
    n_tiles = dest_tiles.shape[0]
    rows = n_tiles * TD * TOP_K
    return pl.pallas_call(
        _dispatch_kernel,
        grid=(n_tiles,),
        in_specs=[pl.BlockSpec((1, 1, TOP_K * TD), lambda i: (i, 0, 0)),
                  pl.BlockSpec((TD * CHUNKS, LANES), lambda i: (i, 0))],
        out_specs=pl.BlockSpec(memory_space=pl.ANY),
        out_shape=jax.ShapeDtypeStruct((rows * CHUNKS, LANES), F32),
        scratch_shapes=[pltpu.SMEM((TOP_K * TD,), I32),
                        pltpu.SemaphoreType.DMA,
                        pltpu.SemaphoreType.DMA],
        compiler_params=_cparams(("arbitrary",)),
        name="dispatch",
    )(dest_tiles, h_rows)


def _experts_kernel(vb_ref, ve_ref, vlo_ref, vhi_ref, vfirst_ref, vseq_ref, nvis_ref,
                    x_ref, wg_ref, wu_ref, wd_ref, out_ref, wgb, wub, wdb, *, n_visits):
    v = pl.program_id(0)
    nxt = jnp.minimum(v + 1, n_visits - 1)

    @pl.when((v == 0) | (ve_ref[nxt] != ve_ref[v]))
    def _():
        s = vseq_ref[nxt] % 2
        wgb[s] = wg_ref[0].astype(BF16)
        wub[s] = wu_ref[0].astype(BF16)

    @pl.when((v == 0) | (ve_ref[v] != ve_ref[jnp.maximum(v - 1, 0)]))
    def _():
        wdb[...] = wd_ref[0].astype(BF16)

    @pl.when((v >= 1) & (v <= nvis_ref[0]))
    def _():
        s = vseq_ref[v] % 2
        y = _swiglu(_load_rows(x_ref, TE), wgb.at[s], wub.at[s], wdb)
        lo, hi = vlo_ref[v], vhi_ref[v]
        whole = (lo == 0) & (hi == TE)

        @pl.when(whole)
        def _():
            _store_rows(out_ref, TE, y)

        @pl.when(jnp.logical_not(whole))
        def _():
            @pl.when(vfirst_ref[v] == 1)
            def _():
                out_ref[...] = jnp.zeros_like(out_ref)

            rows = lax.broadcasted_iota(I32, (TE, 1), 0)
            mine = (rows >= lo) & (rows < hi)
            _store_rows(out_ref, TE, jnp.where(mine, y, _load_rows(out_ref, TE)))


def _experts(visits, xs, w_gate_e, w_up_e, w_down_e):
    n_visits = visits[0].shape[0]
    n_rows = xs.shape[0] // CHUNKS
    ahead = lambda v: jnp.minimum(v + 1, n_visits - 1)
    wspec = lambda shape, at: pl.BlockSpec((1,) + shape, lambda v, vb, ve, *_: (ve[at(v)], 0, 0))
    rows = pl.BlockSpec((TE * CHUNKS, LANES), lambda v, vb, *_: (vb[v], 0))
    grid_spec = pltpu.PrefetchScalarGridSpec(
        num_scalar_prefetch=7,
        grid=(n_visits,),
        in_specs=[rows, wspec((D_MODEL, D_EXPERT), ahead), wspec((D_MODEL, D_EXPERT), ahead),
                  wspec((D_EXPERT, D_MODEL), lambda v: v)],
        out_specs=rows,
        scratch_shapes=[pltpu.VMEM((2, D_MODEL, D_EXPERT), BF16),
                        pltpu.VMEM((2, D_MODEL, D_EXPERT), BF16),
                        pltpu.VMEM((D_EXPERT, D_MODEL), BF16)])
    return pl.pallas_call(
        functools.partial(_experts_kernel, n_visits=n_visits),
        grid_spec=grid_spec,
        out_shape=jax.ShapeDtypeStruct((n_rows * CHUNKS, LANES), F32),
        compiler_params=_cparams(("arbitrary",)),
        name="experts",
    )(*visits, xs, w_gate_e, w_up_e, w_down_e)


def _ffn_out_kernel(cur_ref, nxt_ref, yb_hbm, gates_ref, h_ref, y1_ref, mod_ref, fg_ref,
                    wg_ref, wu_ref, wd_ref, out_ref, idx_smem, gbuf, ffn_ref, isem, gsem,
                    *, n_tiles, tiles_per_row, mod_row0):
    i = pl.program_id(0)
    slot = i % 2
    n_idx = TOP_K * TD

    def gather(idx_vmem, s):
        _stage_indices(idx_vmem, idx_smem, isem)
        _issue_row_dmas(idx_smem, lambda j, tok, src: pltpu.make_async_copy(
            yb_hbm.at[pl.ds(src, CHUNKS)],
            gbuf.at[s, pl.ds(pl.multiple_of((j * TD + tok) * CHUNKS, CHUNKS), CHUNKS)], gsem.at[s]))

    @pl.when(i == 0)
    def _():
        gather(cur_ref, 0)

    @pl.when(i + 1 < n_tiles)
    def _():
        gather(nxt_ref, 1 - slot)

    shared = _swiglu(_load_rows(h_ref, TD), wg_ref, wu_ref, wd_ref)

    pltpu.make_async_copy(yb_hbm.at[pl.ds(0, n_idx * CHUNKS)], gbuf.at[slot], gsem.at[slot]).wait()
    g = gates_ref[...]
    gate = [jnp.broadcast_to(g[:, j:j + 1], (TD, LANES)) for j in range(TOP_K)]
    rows = gbuf.at[slot]
    for c in range(CHUNKS):
        acc = shared[:, c * LANES:(c + 1) * LANES]
        for j in range(TOP_K):
            acc = acc + gate[j] * rows[pl.ds(j * TD * CHUNKS + c, TD, stride=CHUNKS), :]
        ffn_ref[:, c * LANES:(c + 1) * LANES] = acc
    row = mod_row0 + (i // tiles_per_row if tiles_per_row else 0)
    out_ref[...] = y1_ref[...] + _mod_chunk(mod_ref, row, 5) * _rms(ffn_ref[...], fg_ref[...])


def _ffn_out(dest_tiles, yb, gates_t, h_rows, y1, mod, post_g, wg_b, wu_b, wd_b,
             *, tile0, n_tiles, tiles_per_row, mod_row0):
    full = lambda shape: pl.BlockSpec(shape, lambda i: (0,) * len(shape))
    tok = lambda w: pl.BlockSpec((TD, w), lambda i: (tile0 + i, 0))
    last = tile0 + n_tiles - 1
    idx = lambda step: pl.BlockSpec((1, 1, TOP_K * TD),
                                    lambda i: (jnp.minimum(tile0 + i + step, last), 0, 0))
    return pl.pallas_call(
        functools.partial(_ffn_out_kernel, n_tiles=n_tiles,
                          tiles_per_row=tiles_per_row, mod_row0=mod_row0),
        grid=(n_tiles,),
        in_specs=[idx(0), idx(1), pl.BlockSpec(memory_space=pl.ANY),
                  tok(TOP_K), pl.BlockSpec((TD * CHUNKS, LANES), lambda i: (tile0 + i, 0)),
                  tok(D_MODEL), full(mod.shape), full((1, D_MODEL)),
                  full((D_MODEL, D_EXPERT)), full((D_MODEL, D_EXPERT)), full((D_EXPERT, D_MODEL))],
        out_specs=pl.BlockSpec((TD, D_MODEL), lambda i: (i, 0)),
        out_shape=jax.ShapeDtypeStruct((n_tiles * TD, D_MODEL), F32),
        scratch_shapes=[pltpu.SMEM((TOP_K * TD,), I32),
                        pltpu.VMEM((2, TOP_K * TD * CHUNKS, LANES), F32),
                        pltpu.VMEM((TD, D_MODEL), F32),
                        pltpu.SemaphoreType.DMA,
                        pltpu.SemaphoreType.DMA((2,))],
        compiler_params=_cparams(("arbitrary",)),
        name="ffn_out",
    )(dest_tiles, dest_tiles, yb, gates_t, h_rows, y1, mod, post_g, wg_b, wu_b, wd_b)


def _sorted_rows_kernel(ids_ref, dest_ref, counts_ref, carry):
    p, i = pl.program_id(0), pl.program_id(1)
    tm = ids_ref.shape[1]
    ids = ids_ref[...]
    expert = lax.broadcasted_iota(I32, (N_EXPERTS, tm), 0)
    chose = [ids[j:j + 1, :] == expert for j in range(TOP_K)]
    onehot = sum(jnp.where(c, 1.0, 0.0) for c in chose)
    tile_counts = jnp.broadcast_to(onehot.sum(axis=1, keepdims=True), (N_EXPERTS, LANES))

    @pl.when((p == 0) & (i == 0))
    def _():
        carry[...] = jnp.zeros_like(carry)

    @pl.when(p == 0)
    def _():
        carry[...] += tile_counts

    @pl.when((p == 1) & (i == 0))
    def _():
        counts = carry[...]
        counts_ref[...] = counts
        row = lax.broadcasted_iota(I32, (N_EXPERTS, LANES), 0)
        incl = counts
        shift = 1
        while shift < N_EXPERTS:
            incl = incl + jnp.where(row >= shift, pltpu.roll(incl, shift, 0), 0.0)
            shift *= 2
        carry[...] = incl - counts

    @pl.when(p == 1)
    def _():
        earlier = (lax.broadcasted_iota(I32, (tm, tm), 0) < lax.broadcasted_iota(I32, (tm, tm), 1))
        before = jnp.dot(onehot.astype(BF16), jnp.where(earlier, 1.0, 0.0).astype(BF16),
                         preferred_element_type=F32)
        pos = carry[:, 0:1] + before
        rows = [jnp.where(c, pos, 0.0).sum(axis=0, keepdims=True) for c in chose]
        dest_ref[...] = jnp.concatenate(rows, axis=0).astype(I32)
        carry[...] += tile_counts


def _sorted_rows(ids):
    t = ids.shape[1]
    n_tiles = t // TM
    return pl.pallas_call(
        _sorted_rows_kernel,
        grid=(2, n_tiles),
        in_specs=[pl.BlockSpec((TOP_K, TM), lambda p, i: (0, i))],
        out_specs=[pl.BlockSpec((TOP_K, TM), lambda p, i: (0, i * p)),
                   pl.BlockSpec((N_EXPERTS, LANES), lambda p, i: (0, 0))],
        out_shape=[jax.ShapeDtypeStruct((TOP_K, t), I32),
                   jax.ShapeDtypeStruct((N_EXPERTS, LANES), F32)],
        scratch_shapes=[pltpu.VMEM((N_EXPERTS, LANES), F32)],
        compiler_params=_cparams(("arbitrary", "arbitrary")),
        name="sorted_rows",
    )(ids)


def _routing_tables(ids):
    t = ids.shape[1]
    n_blocks = t * TOP_K // TE
    n_visits = n_blocks + N_EXPERTS - 1
    dest, counts = _sorted_rows(ids)
    counts = counts[:, 0].astype(I32)
    end = jnp.cumsum(counts)
    start = end - counts

    first_blk = start // TE
    n_vis_e = jnp.where(counts > 0, (end - 1) // TE - first_blk + 1, 0)
    vend = jnp.cumsum(n_vis_e)
    vstart = vend - n_vis_e
    n_vis = vend[-1]
    v = jnp.arange(n_visits, dtype=I32)
    valid = v < n_vis
    mine = (vstart[None, :] <= v[:, None]) & (v[:, None] < vend[None, :])
    pick = lambda x: jnp.where(mine, x[None, :], 0).sum(axis=1)
    experts = jnp.arange(N_EXPERTS, dtype=I32)
    last_expert = jnp.max(jnp.where(n_vis_e > 0, experts, 0))
    ve = jnp.where(valid, pick(experts), last_expert).astype(I32)
    vb = jnp.where(valid, pick(first_blk) + v - pick(vstart), n_blocks - 1).astype(I32)
    lo = jnp.where(valid, jnp.clip(pick(start) - vb * TE, 0, TE), 0).astype(I32)
    hi = jnp.where(valid, jnp.clip(pick(end) - vb * TE, 0, TE), 0).astype(I32)
    vfirst = (valid & ((v == 0) | (vb != jnp.roll(vb, 1)))).astype(I32)
    lead = lambda x, first: jnp.concatenate([first, x])
    zero = jnp.zeros((1,), I32)
    vb, ve = lead(vb, vb[:1]), lead(ve, ve[:1])
    lo, hi, vfirst = lead(lo, zero), lead(hi, zero), lead(vfirst, zero)
    changed = (ve != jnp.roll(ve, 1)) & (jnp.arange(ve.shape[0]) > 0)
    vseq = jnp.cumsum(changed.astype(I32))
    return dest.astype(I32), (vb, ve, lo, hi, vfirst, vseq.astype(I32), n_vis.reshape(1).astype(I32))


def _rope_tables(n_tokens):
    rows = n_tokens // GRID_W
    row = jnp.repeat(jnp.arange(rows, dtype=F32), GRID_W)
    col = jnp.tile(jnp.arange(GRID_W, dtype=F32), rows)
    inv_freq = ROPE_THETA ** (-jnp.arange(0, AXIS_DIM, 2, dtype=F32) / AXIS_DIM)
    ang_r = row[:, None] * inv_freq
    ang_c = col[:, None] * inv_freq
    ang = jnp.concatenate([ang_r, ang_r, ang_c, ang_c], axis=-1)
    cos, sin = jnp.cos(ang), jnp.sin(ang)
    first = (jnp.arange(HEAD_DIM) % AXIS_DIM) < (AXIS_DIM // 2)
    return cos, jnp.where(first, -sin, 0.0), jnp.where(first, 0.0, sin)


def kernel(x_prompt, x_sample, cache_glob_k, cache_glob_v, cache_win_k, cache_win_v, c, c_ctx,
           w_ada, b_ada, attn_pre_g, attn_post_g, w_in, q_norm_g, k_norm_g, sink_logit, w_out,
           ffn_pre_g, ffn_post_g, w_router, router_bias, w_gate_e, w_up_e, w_down_e,
           w_gate_s, w_up_s, w_down_s):
    batch, seq, d = x_prompt.shape
    dec_batch, dec_seq, _ = x_sample.shape
    past = cache_glob_k.shape[2]
    tp, ts = batch * seq, dec_batch * dec_seq
    t = tp + ts
    l = 0

    xp = x_prompt.reshape(tp, d)
    xs = x_sample.reshape(ts, d)
    cond = jnp.concatenate([c_ctx[None, :], c, jnp.zeros((MOD_ROWS - 1 - dec_batch, d), F32)], axis=0)
    mod = _adaln(cond, w_ada[l], b_ada[l][None, :])

    w_in_b = w_in[l].astype(BF16)
    pre_g = attn_pre_g[l][None, :]
    qn, kn = q_norm_g[l][None, :], k_norm_g[l][None, :]
    cos, sa, sb = _rope_tables(dec_seq)
    sink = sink_logit[l]

    qg, kg, vg, qw, kw, vw = _qkv(xp, mod, pre_g, w_in_b, qn, kn, cos, sa, sb,
                                  rope=False, seq=seq, kv_dtype=F32)
    o_p = _ctx_attention(sink, qg, kg, vg, qw, kw, vw, seq=seq)
    new_kv = [a.reshape(batch, 1, seq, N_KV, HEAD_DIM) for a in (kg, vg, kw, vw)]

    qg, kg, vg, qw, kw, vw = _qkv(xs, mod, pre_g, w_in_b, qn, kn, cos, sa, sb,
                                  rope=True, seq=dec_seq, kv_dtype=BF16)
    caches = [a[:, l].reshape(dec_batch * past, KV_W)
              for a in (cache_glob_k, cache_glob_v, cache_win_k, cache_win_v)]
    o_s = _lat_attention(sink, qg, kg, vg, qw, kw, vw, *caches, seq=dec_seq, past=past)

    wr_t = w_router[l].T
    wr_hi = wr_t.astype(BF16)
    wr_lo = (wr_t - wr_hi.astype(F32)).astype(BF16)
    y1, h_rows, ids, gates = _post_attn(
        o_p, o_s, xp, xs, mod, w_out[l].astype(BF16), attn_post_g[l][None, :], ffn_pre_g[l][None, :],
        wr_hi, wr_lo, router_bias[l][:, None], seq=dec_seq)

    dest, visits = _routing_tables(ids)
    dest_tiles = (dest * CHUNKS).reshape(TOP_K, t // TD, TD).transpose(1, 0, 2).reshape(t // TD, 1, TOP_K * TD)
    xs_rows = _dispatch(dest_tiles, h_rows)
    yb = _experts(visits, xs_rows, w_gate_e[l], w_up_e[l], w_down_e[l])

    gates_t = gates.T
    shared = (w_gate_s[l].astype(BF16), w_up_s[l].astype(BF16), w_down_s[l].astype(BF16))
    post_g = ffn_post_g[l][None, :]
    y_p = _ffn_out(dest_tiles, yb, gates_t, h_rows, y1, mod, post_g, *shared,
                   tile0=0, n_tiles=tp // TD, tiles_per_row=0, mod_row0=0)
    y_s = _ffn_out(dest_tiles, yb, gates_t, h_rows, y1, mod, post_g, *shared,
                   tile0=tp // TD, n_tiles=ts // TD, tiles_per_row=dec_seq // TD, mod_row0=1)
    return (y_p.reshape(batch, seq, d), y_s.reshape(dec_batch, dec_seq, d), *new_kv)
```

```python
import functools

import jax
import jax.numpy as jnp
from jax import lax
from jax.experimental import pallas as pl
from jax.experimental.pallas import tpu as pltpu

F32 = jnp.float32
BF16 = jnp.bfloat16
I32 = jnp.int32

D_MODEL = 2048
GRID_W = 64
HEAD_DIM = 128
AXIS_DIM = HEAD_DIM // 2
N_HEADS = 8
N_KV = 2
GROUP = N_HEADS // N_KV
Q_W = N_HEADS * HEAD_DIM
KV_W = N_KV * HEAD_DIM
QKV_WIDTH = 2 * Q_W + 4 * KV_W
WINDOW = 128
ROPE_THETA = 10000.0
NORM_EPS = 1e-6
N_EXPERTS = 64
N_GROUPS = 8
TOPK_GROUPS = 4
TOP_K = 8
D_EXPERT = 512
ROUTED_SCALE = 2.5

LANES = 128
CHUNKS = D_MODEL // LANES
MOD_ROWS = 8
VMEM_LIMIT = 56 * 1024 * 1024

TM = 256
TD = 128
TQ = 256
TE = 256
NEG = -1e30
ISSUE_UNROLL = 32
DMA_THREADS = 2


def _cparams(sem):
    return pltpu.CompilerParams(dimension_semantics=sem, vmem_limit_bytes=VMEM_LIMIT)


def _store_rows(ref, n, x):
    for c in range(CHUNKS):
        ref[pl.ds(c, n, stride=CHUNKS), :] = x[:, c * LANES:(c + 1) * LANES]


def _load_rows(ref, n, first=0):
    return jnp.concatenate(
        [ref[pl.ds(first * CHUNKS + c, n, stride=CHUNKS), :] for c in range(CHUNKS)], axis=-1)


def _rms(x, gain):
    r = lax.rsqrt(jnp.mean(x * x, axis=-1, keepdims=True) + NORM_EPS)
    return x * r * gain


def _dot_nt(a, b):
    return lax.dot_general(a, b, (((1,), (1,)), ((), ())), preferred_element_type=F32)


def _swiglu(x, wg, wu, wd):
    xb = x.astype(BF16)
    gate = jnp.dot(xb, wg[...], preferred_element_type=F32)
    up = jnp.dot(xb, wu[...], preferred_element_type=F32)
    act = (gate / (1.0 + jnp.exp(-gate)) * up).astype(BF16)
    return jnp.dot(act, wd[...], preferred_element_type=F32)


def _adaln_kernel(cond_ref, w_ref, b_ref, o_ref):
    c = cond_ref[...]
    s = (c / (1.0 + jnp.exp(-c))).astype(BF16)
    o_ref[...] = jnp.dot(s, w_ref[...].astype(BF16), preferred_element_type=F32) + b_ref[...]


def _adaln(cond, w_ada, b_ada):
    n = w_ada.shape[1]
    tn = 1024
    return pl.pallas_call(
        _adaln_kernel,
        grid=(n // tn,),
        in_specs=[pl.BlockSpec((MOD_ROWS, D_MODEL), lambda j: (0, 0)),
                  pl.BlockSpec((D_MODEL, tn), lambda j: (0, j)),
                  pl.BlockSpec((1, tn), lambda j: (0, j))],
        out_specs=pl.BlockSpec((MOD_ROWS, tn), lambda j: (0, j)),
        out_shape=jax.ShapeDtypeStruct((MOD_ROWS, n), F32),
        compiler_params=_cparams(("arbitrary",)),
        name="adaln",
    )(cond, w_ada, b_ada)


def _mod_chunk(mod_ref, row, k):
    return mod_ref[pl.ds(row, 1), k * D_MODEL:(k + 1) * D_MODEL]


def _qkv_kernel(x_ref, mod_ref, g_ref, w_ref, qn_ref, kn_ref, cos_ref, sa_ref, sb_ref,
                qg_ref, kg_ref, vg_ref, qw_ref, kw_ref, vw_ref, *, rope, tiles_per_row):
    i = pl.program_id(0)
    row = (1 + i // tiles_per_row) if rope else 0
    x = x_ref[...]
    h = _rms(x, g_ref[...]) * (1.0 + _mod_chunk(mod_ref, row, 1)) + _mod_chunk(mod_ref, row, 0)
    proj = jnp.dot(h.astype(BF16), w_ref[...], preferred_element_type=F32)

    if rope:
        cos, sa, sb = cos_ref[...], sa_ref[...], sb_ref[...]

    def rot(t):
        if not rope:
            return t
        return t * cos + pltpu.roll(t, 96, 1) * sa + pltpu.roll(t, 32, 1) * sb

    scale = HEAD_DIM ** -0.5
    qn, kn = qn_ref[...], kn_ref[...]
    off = 0
    for hd in range(N_HEADS):
        t = proj[:, off + hd * HEAD_DIM: off + (hd + 1) * HEAD_DIM]
        qg_ref[:, hd * HEAD_DIM:(hd + 1) * HEAD_DIM] = (rot(_rms(t, qn)) * scale).astype(qg_ref.dtype)
    off += Q_W
    for hd in range(N_KV):
        t = proj[:, off + hd * HEAD_DIM: off + (hd + 1) * HEAD_DIM]
        kg_ref[:, hd * HEAD_DIM:(hd + 1) * HEAD_DIM] = rot(_rms(t, kn)).astype(kg_ref.dtype)
    off += KV_W
    vg_ref[...] = proj[:, off:off + KV_W].astype(vg_ref.dtype)
    off += KV_W
    for hd in range(N_HEADS):
        t = proj[:, off + hd * HEAD_DIM: off + (hd + 1) * HEAD_DIM]
        qw_ref[:, hd * HEAD_DIM:(hd + 1) * HEAD_DIM] = (rot(t) * scale).astype(qw_ref.dtype)
    off += Q_W
    for hd in range(N_KV):
        t = proj[:, off + hd * HEAD_DIM: off + (hd + 1) * HEAD_DIM]
        kw_ref[:, hd * HEAD_DIM:(hd + 1) * HEAD_DIM] = rot(t).astype(kw_ref.dtype)
    off += KV_W
    vw_ref[...] = proj[:, off:off + KV_W].astype(vw_ref.dtype)


def _qkv(x2d, mod, pre_g, w_in_b, qn, kn, cos, sa, sb, *, rope, seq, kv_dtype):
    t = x2d.shape[0]
    tiles_per_row = seq // TM
    full = lambda shape: pl.BlockSpec(shape, lambda i: (0,) * len(shape))
    tab = pl.BlockSpec((TM, HEAD_DIM), (lambda i: (i % tiles_per_row, 0)) if rope else (lambda i: (0, 0)))
    tok = lambda w: pl.BlockSpec((TM, w), lambda i: (i, 0))
    return pl.pallas_call(
        functools.partial(_qkv_kernel, rope=rope, tiles_per_row=tiles_per_row),
        grid=(t // TM,),
        in_specs=[tok(D_MODEL), full(mod.shape), full((1, D_MODEL)),
                  full((D_MODEL, QKV_WIDTH)), full((1, HEAD_DIM)), full((1, HEAD_DIM)),
                  tab, tab, tab],
        out_specs=[tok(Q_W), tok(KV_W), tok(KV_W), tok(Q_W), tok(KV_W), tok(KV_W)],
        out_shape=[jax.ShapeDtypeStruct((t, Q_W), BF16),
                   jax.ShapeDtypeStruct((t, KV_W), kv_dtype),
                   jax.ShapeDtypeStruct((t, KV_W), kv_dtype),
                   jax.ShapeDtypeStruct((t, Q_W), BF16),
                   jax.ShapeDtypeStruct((t, KV_W), kv_dtype),
                   jax.ShapeDtypeStruct((t, KV_W), kv_dtype)],
        compiler_params=_cparams(("arbitrary",)),
        name="qkv_rope" if rope else "qkv_ctx",
    )(x2d, mod, pre_g, w_in_b, qn, kn, cos, sa, sb)


def _softmax_pv(scores, values, sink):
    m = scores[0].max(axis=-1, keepdims=True)
    for s in scores[1:]:
        m = jnp.maximum(m, s.max(axis=-1, keepdims=True))
    if sink is not None:
        m = jnp.maximum(m, sink)
    den = jnp.exp(sink - m) if sink is not None else 0.0
    acc = None
    for s, v in zip(scores, values):
        p = jnp.exp(s - m)
        den = den + p.sum(axis=-1, keepdims=True)
        pv = jnp.dot(p.astype(BF16), v, preferred_element_type=F32)
        acc = pv if acc is None else acc + pv
    return acc / den


def _ctx_attn_kernel(sink_ref, qg_ref, kg_ref, vg_ref, qw_ref, kw_ref, vw_ref, o_ref):
    for mixer, (q_ref, k_ref, v_ref) in enumerate(((qg_ref, kg_ref, vg_ref), (qw_ref, kw_ref, vw_ref))):
        for n in range(N_KV):
            k = k_ref[:, n * HEAD_DIM:(n + 1) * HEAD_DIM].astype(BF16)
            v = v_ref[:, n * HEAD_DIM:(n + 1) * HEAD_DIM].astype(BF16)
            for g in range(GROUP):
                hd = n * GROUP + g
                q = q_ref[:, hd * HEAD_DIM:(hd + 1) * HEAD_DIM]
                sink = sink_ref[hd] if mixer == 1 else None
                o = _softmax_pv([_dot_nt(q, k)], [v], sink)
                c0 = mixer * Q_W + hd * HEAD_DIM
                o_ref[:, c0:c0 + HEAD_DIM] = o.astype(o_ref.dtype)


def _ctx_attention(sink, qg, kg, vg, qw, kw, vw, *, seq):
    t = qg.shape[0]
    blk = lambda w: pl.BlockSpec((seq, w), lambda b: (b, 0))
    return pl.pallas_call(
        _ctx_attn_kernel,
        grid=(t // seq,),
        in_specs=[pl.BlockSpec(memory_space=pltpu.SMEM),
                  blk(Q_W), blk(KV_W), blk(KV_W), blk(Q_W), blk(KV_W), blk(KV_W)],
        out_specs=blk(2 * Q_W),
        out_shape=jax.ShapeDtypeStruct((t, 2 * Q_W), BF16),
        compiler_params=_cparams(("arbitrary",)),
        name="attn_ctx",
    )(sink, qg, kg, vg, qw, kw, vw)


def _lat_attn_kernel(sink_ref, qg_ref, kg_ref, vg_ref, qw_ref, kw_ref, vw_ref,
                     cgk_ref, cgv_ref, cwk_ref, cwv_ref, o_ref, *, seq):
    qi = pl.program_id(1)
    span = TQ + 2 * WINDOW
    start = pl.multiple_of(jnp.clip(qi * TQ - WINDOW, 0, seq - span), WINDOW)
    qpos = qi * TQ + lax.broadcasted_iota(I32, (TQ, span), 0)
    kpos = start + lax.broadcasted_iota(I32, (TQ, span), 1)
    band_ok = jnp.abs(kpos - qpos) <= WINDOW
    for n in range(N_KV):
        cols = slice(n * HEAD_DIM, (n + 1) * HEAD_DIM)
        kc, vc = cgk_ref[:, cols].astype(BF16), cgv_ref[:, cols].astype(BF16)
        kl, vl = kg_ref[:, cols], vg_ref[:, cols]
        for g in range(GROUP):
            hd = n * GROUP + g
            q = qg_ref[:, hd * HEAD_DIM:(hd + 1) * HEAD_DIM]
            o = _softmax_pv([_dot_nt(q, kc), _dot_nt(q, kl)], [vc, vl], None)
            o_ref[:, hd * HEAD_DIM:(hd + 1) * HEAD_DIM] = o.astype(o_ref.dtype)
        kc, vc = cwk_ref[:, cols].astype(BF16), cwv_ref[:, cols].astype(BF16)
        kl, vl = kw_ref[pl.ds(start, span), cols], vw_ref[pl.ds(start, span), cols]
        for g in range(GROUP):
            hd = n * GROUP + g
            q = qw_ref[:, hd * HEAD_DIM:(hd + 1) * HEAD_DIM]
            band = jnp.where(band_ok, _dot_nt(q, kl), NEG)
            o = _softmax_pv([_dot_nt(q, kc), band], [vc, vl], sink_ref[hd])
            c0 = Q_W + hd * HEAD_DIM
            o_ref[:, c0:c0 + HEAD_DIM] = o.astype(o_ref.dtype)


def _lat_attention(sink, qg, kg, vg, qw, kw, vw, cgk, cgv, cwk, cwv, *, seq, past):
    t = qg.shape[0]
    nq = seq // TQ
    qblk = pl.BlockSpec((TQ, Q_W), lambda b, i: (b * nq + i, 0))
    kblk = pl.BlockSpec((seq, KV_W), lambda b, i: (b, 0))
    cblk = pl.BlockSpec((past, KV_W), lambda b, i: (b, 0))
    return pl.pallas_call(
        functools.partial(_lat_attn_kernel, seq=seq),
        grid=(t // seq, nq),
        in_specs=[pl.BlockSpec(memory_space=pltpu.SMEM),
                  qblk, kblk, kblk, qblk, kblk, kblk, cblk, cblk, cblk, cblk],
        out_specs=pl.BlockSpec((TQ, 2 * Q_W), lambda b, i: (b * nq + i, 0)),
        out_shape=jax.ShapeDtypeStruct((t, 2 * Q_W), BF16),
        compiler_params=_cparams(("arbitrary", "arbitrary")),
        name="attn_lat",
    )(sink, qg, kg, vg, qw, kw, vw, cgk, cgv, cwk, cwv)


def _route(sel, scores, tm):
    per = N_EXPERTS // N_GROUPS
    sel3 = sel.reshape(N_GROUPS, per, tm)
    sc3 = scores.reshape(N_GROUPS, per, tm)
    member = lax.broadcasted_iota(I32, (N_GROUPS, per, tm), 1)
    m1 = sel3.max(axis=1, keepdims=True)
    first = jnp.where(sel3 == m1, member, per).min(axis=1, keepdims=True)
    m2 = jnp.where(member == first, -jnp.inf, sel3).max(axis=1, keepdims=True)
    gs = m1 + m2

    gid = lax.broadcasted_iota(I32, (N_GROUPS, 1, tm), 0)
    chosen = jnp.zeros((N_GROUPS, 1, tm), F32)
    for _ in range(TOPK_GROUPS):
        gm = gs.max(axis=0, keepdims=True)
        gfirst = jnp.where(gs == gm, gid, N_GROUPS).min(axis=0, keepdims=True)
        hit = gid == gfirst
        chosen = jnp.where(hit, 1.0, chosen)
        gs = jnp.where(hit, -jnp.inf, gs)

    eid = lax.broadcasted_iota(I32, (N_GROUPS, per, tm), 0) * per + member
    masked = jnp.where(chosen > 0.0, sel3, -jnp.inf)
    ids, raw = [], []
    for _ in range(TOP_K):
        mx = masked.max(axis=1, keepdims=True).max(axis=0, keepdims=True)
        efirst = jnp.where(masked == mx, eid, N_EXPERTS).min(axis=1, keepdims=True).min(axis=0, keepdims=True)
        hit = eid == efirst
        ids.append(efirst.reshape(1, tm))
        raw.append(jnp.where(hit, sc3, 0.0).sum(axis=1, keepdims=True).sum(axis=0, keepdims=True).reshape(1, tm))
        masked = jnp.where(hit, -jnp.inf, masked)
    ids = jnp.concatenate(ids, axis=0)
    raw = jnp.concatenate(raw, axis=0)
    gates = raw / raw.sum(axis=0, keepdims=True) * ROUTED_SCALE
    return ids, gates


def _post_attn_kernel(op_ref, os_ref, xp_ref, xs_ref, mod_ref, w_ref, pg_ref, fg_ref,
                      wrh_ref, wrl_ref, rb_ref,
                      y_ref, h_ref, ids_ref, gates_ref, *, ctx_tiles, tiles_per_row):
    i = pl.program_id(0)
    is_ctx = i < ctx_tiles
    row = jnp.where(is_ctx, 0, 1 + (i - ctx_tiles) // tiles_per_row)
    o = jnp.where(is_ctx, op_ref[...], os_ref[...])
    x = jnp.where(is_ctx, xp_ref[...], xs_ref[...])
    a = jnp.dot(o, w_ref[...], preferred_element_type=F32)
    y = x + _mod_chunk(mod_ref, row, 2) * _rms(a, pg_ref[...])
    y_ref[...] = y
    h = _rms(y, fg_ref[...]) * (1.0 + _mod_chunk(mod_ref, row, 4)) + _mod_chunk(mod_ref, row, 3)
    _store_rows(h_ref, h.shape[0], h)
    h_hi = h.astype(BF16)
    h_lo = (h - h_hi.astype(F32)).astype(BF16)
    wh, wl = wrh_ref[...], wrl_ref[...]
    logits = _dot_nt(wh, h_hi) + (_dot_nt(wl, h_hi) + _dot_nt(wh, h_lo))
    scores = 1.0 / (1.0 + jnp.exp(-logits))
    ids, gates = _route(scores + rb_ref[...], scores, logits.shape[1])
    ids_ref[...] = ids
    gates_ref[...] = gates


def _post_attn(o_p, o_s, x_p, x_s, mod, w_out_b, post_g, ffn_g, wr_hi, wr_lo, rbias, *, seq):
    tp, ts = x_p.shape[0], x_s.shape[0]
    t = tp + ts
    ctx_tiles = tp // TM
    full = lambda shape: pl.BlockSpec(shape, lambda i: (0,) * len(shape))
    ctx = lambda w: pl.BlockSpec((TM, w), lambda i: (jnp.minimum(i, ctx_tiles - 1), 0))
    lat = lambda w: pl.BlockSpec((TM, w), lambda i: (jnp.maximum(i - ctx_tiles, 0), 0))
    tok = lambda w: pl.BlockSpec((TM, w), lambda i: (i, 0))
    lane = pl.BlockSpec((TOP_K, TM), lambda i: (0, i))
    return pl.pallas_call(
        functools.partial(_post_attn_kernel, ctx_tiles=ctx_tiles, tiles_per_row=seq // TM),
        grid=(t // TM,),
        in_specs=[ctx(D_MODEL), lat(D_MODEL), ctx(D_MODEL), lat(D_MODEL), full(mod.shape),
                  full((D_MODEL, D_MODEL)), full((1, D_MODEL)), full((1, D_MODEL)),
                  full((N_EXPERTS, D_MODEL)), full((N_EXPERTS, D_MODEL)), full((N_EXPERTS, 1))],
        out_specs=[tok(D_MODEL), pl.BlockSpec((TM * CHUNKS, LANES), lambda i: (i, 0)), lane, lane],
        out_shape=[jax.ShapeDtypeStruct((t, D_MODEL), F32),
                   jax.ShapeDtypeStruct((t * CHUNKS, LANES), F32),
                   jax.ShapeDtypeStruct((TOP_K, t), I32),
                   jax.ShapeDtypeStruct((TOP_K, t), F32)],
        compiler_params=_cparams(("arbitrary",)),
        name="post_attn_router",
    )(o_p, o_s, x_p, x_s, mod, w_out_b, post_g, ffn_g, wr_hi, wr_lo, rbias)


def _stage_indices(idx_vmem, idx_smem, sem):
    cp = pltpu.make_async_copy(idx_vmem.at[0, 0], idx_smem, sem)
    cp.start()
    cp.wait()


def _issue_row_dmas(idx_smem, copy_of):
    def slot_body(j, carry):
        def chunk(c, carry):
            for u in range(ISSUE_UNROLL):
                tok = c * ISSUE_UNROLL + u
                copy_of(j, tok, pl.multiple_of(idx_smem[j * TD + tok], CHUNKS)).start(
                    priority=u % DMA_THREADS)
            return carry
        return lax.fori_loop(0, TD // ISSUE_UNROLL, chunk, carry)
    lax.fori_loop(0, TOP_K, slot_body, 0)


def _dispatch_kernel(idx_ref, h_ref, xs_hbm, idx_smem, isem, dsem):
    _stage_indices(idx_ref, idx_smem, isem)

    def copy_of(j, tok, dst):
        src = pl.multiple_of(tok * CHUNKS, CHUNKS)
        return pltpu.make_async_copy(h_ref.at[pl.ds(src, CHUNKS)], xs_hbm.at[pl.ds(dst, CHUNKS)], dsem)

    _issue_row_dmas(idx_smem, copy_of)
    for _ in range(TOP_K):
        pltpu.make_async_copy(h_ref, xs_hbm.at[pl.ds(0, TD * CHUNKS)], dsem).wait()


def _dispatch(dest_tiles, h_rows):
    n_tiles = dest_tiles.shape[0]
    rows = n_tiles * TD * TOP_K
    return pl.pallas_call(
        _dispatch_kernel,
        grid=(n_tiles,),
        in_specs=[pl.BlockSpec((1, 1, TOP_K * TD), lambda i: (i, 0, 0)),
                  pl.BlockSpec((TD * CHUNKS, LANES), lambda i: (i, 0))],
        out_specs=pl.BlockSpec(memory_space=pl.ANY),
        out_shape=jax.ShapeDtypeStruct((rows * CHUNKS, LANES), F32),
        scratch_shapes=[pltpu.SMEM((TOP_K * TD,), I32),
                        pltpu.SemaphoreType.DMA,
                        pltpu.SemaphoreType.DMA],
        compiler_params=_cparams(("arbitrary",)),
        name="dispatch",
    )(dest_tiles, h_rows)


def _experts_kernel(vb_ref, ve_ref, vlo_ref, vhi_ref, vfirst_ref, vseq_ref, nvis_ref,
                    x_ref, wg_ref, wu_ref, wd_ref, out_ref, wgb, wub, wdb, *, n_visits):
    v = pl.program_id(0)
    nxt = jnp.minimum(v + 1, n_visits - 1)

    @pl.when((v == 0) | (ve_ref[nxt] != ve_ref[v]))
    def _():
        s = vseq_ref[nxt] % 2
        wgb[s] = wg_ref[0].astype(BF16)
        wub[s] = wu_ref[0].astype(BF16)

    @pl.when((v == 0) | (ve_ref[v] != ve_ref[jnp.maximum(v - 1, 0)]))
    def _():
        wdb[...] = wd_ref[0].astype(BF16)

    @pl.when((v >= 1) & (v <= nvis_ref[0]))
    def _():
        s = vseq_ref[v] % 2
        slabs = jnp.swapaxes(x_ref[...].reshape(TE, CHUNKS, LANES), 0, 1)
        x = jnp.concatenate([slabs[c] for c in range(CHUNKS)], axis=-1)
        y = _swiglu(x, wgb.at[s], wub.at[s], wdb)
        lo, hi = vlo_ref[v], vhi_ref[v]
        whole = (lo == 0) & (hi == TE)

        @pl.when(whole)
        def _():
            slabs = jnp.stack([y[:, c * LANES:(c + 1) * LANES] for c in range(CHUNKS)], axis=0)
            out_ref[...] = jnp.swapaxes(slabs, 0, 1).reshape(TE * CHUNKS, LANES)

        @pl.when(jnp.logical_not(whole))
        def _():
            @pl.when(vfirst_ref[v] == 1)
            def _():
                out_ref[...] = jnp.zeros_like(out_ref)

            rows = lax.broadcasted_iota(I32, (TE, 1), 0)
            mine = (rows >= lo) & (rows < hi)
            _store_rows(out_ref, TE, jnp.where(mine, y, _load_rows(out_ref, TE)))


def _experts(visits, xs, w_gate_e, w_up_e, w_down_e):
    n_visits = visits[0].shape[0]
    n_rows = xs.shape[0] // CHUNKS
    ahead = lambda v: jnp.minimum(v + 1, n_visits - 1)
    wspec = lambda shape, at: pl.BlockSpec((1,) + shape, lambda v, vb, ve, *_: (ve[at(v)], 0, 0))
    rows = pl.BlockSpec((TE * CHUNKS, LANES), lambda v, vb, *_: (vb[v], 0))
    grid_spec = pltpu.PrefetchScalarGridSpec(
        num_scalar_prefetch=7,
        grid=(n_visits,),
        in_specs=[rows, wspec((D_MODEL, D_EXPERT), ahead), wspec((D_MODEL, D_EXPERT), ahead),
                  wspec((D_EXPERT, D_MODEL), lambda v: v)],
        out_specs=rows,
        scratch_shapes=[pltpu.VMEM((2, D_MODEL, D_EXPERT), BF16),
                        pltpu.VMEM((2, D_MODEL, D_EXPERT), BF16),
                        pltpu.VMEM((D_EXPERT, D_MODEL), BF16)])
    return pl.pallas_call(
        functools.partial(_experts_kernel, n_visits=n_visits),
        grid_spec=grid_spec,
        out_shape=jax.ShapeDtypeStruct((n_rows * CHUNKS, LANES), F32),
        compiler_params=_cparams(("arbitrary",)),
        name="experts",
    )(*visits, xs, w_gate_e, w_up_e, w_down_e)


def _ffn_out_kernel(cur_ref, nxt_ref, yb_hbm, gates_ref, h_ref, y1_ref, mod_ref, fg_ref,
                    wg_ref, wu_ref, wd_ref, out_ref, idx_smem, gbuf, ffn_ref, isem, gsem,
                    *, n_tiles, tiles_per_row, mod_row0):
    i = pl.program_id(0)
    slot = i % 2
    n_idx = TOP_K * TD

    def gather(idx_vmem, s):
        _stage_indices(idx_vmem, idx_smem, isem)
        _issue_row_dmas(idx_smem, lambda j, tok, src: pltpu.make_async_copy(
            yb_hbm.at[pl.ds(src, CHUNKS)],
            gbuf.at[s, pl.ds(pl.multiple_of((j * TD + tok) * CHUNKS, CHUNKS), CHUNKS)], gsem.at[s]))

    @pl.when(i == 0)
    def _():
        gather(cur_ref, 0)

    @pl.when(i + 1 < n_tiles)
    def _():
        gather(nxt_ref, 1 - slot)

    shared = _swiglu(_load_rows(h_ref, TD), wg_ref, wu_ref, wd_ref)

    pltpu.make_async_copy(yb_hbm.at[pl.ds(0, n_idx * CHUNKS)], gbuf.at[slot], gsem.at[slot]).wait()
    g = gates_ref[...]
    gate = [jnp.broadcast_to(g[:, j:j + 1], (TD, LANES)) for j in range(TOP_K)]
    rows = gbuf.at[slot]
    for c in range(CHUNKS):
        acc = shared[:, c * LANES:(c + 1) * LANES]
        for j in range(TOP_K):
            acc = acc + gate[j] * rows[pl.ds(j * TD * CHUNKS + c, TD, stride=CHUNKS), :]
        ffn_ref[:, c * LANES:(c + 1) * LANES] = acc
    row = mod_row0 + (i // tiles_per_row if tiles_per_row else 0)
    out_ref[...] = y1_ref[...] + _mod_chunk(mod_ref, row, 5) * _rms(ffn_ref[...], fg_ref[...])


def _ffn_out(dest_tiles, yb, gates_t, h_rows, y1, mod, post_g, wg_b, wu_b, wd_b,
             *, tile0, n_tiles, tiles_per_row, mod_row0):
    full = lambda shape: pl.BlockSpec(shape, lambda i: (0,) * len(shape))
    tok = lambda w: pl.BlockSpec((TD, w), lambda i: (tile0 + i, 0))
    last = tile0 + n_tiles - 1
    idx = lambda step: pl.BlockSpec((1, 1, TOP_K * TD),
                                    lambda i: (jnp.minimum(tile0 + i + step, last), 0, 0))
    return pl.pallas_call(
        functools.partial(_ffn_out_kernel, n_tiles=n_tiles,
                          tiles_per_row=tiles_per_row, mod_row0=mod_row0),
        grid=(n_tiles,),
        in_specs=[idx(0), idx(1), pl.BlockSpec(memory_space=pl.ANY),
                  tok(TOP_K), pl.BlockSpec((TD * CHUNKS, LANES), lambda i: (tile0 + i, 0)),
                  tok(D_MODEL), full(mod.shape), full((1, D_MODEL)),
                  full((D_MODEL, D_EXPERT)), full((D_MODEL, D_EXPERT)), full((D_EXPERT, D_MODEL))],
        out_specs=pl.BlockSpec((TD, D_MODEL), lambda i: (i, 0)),
        out_shape=jax.ShapeDtypeStruct((n_tiles * TD, D_MODEL), F32),
        scratch_shapes=[pltpu.SMEM((TOP_K * TD,), I32),
                        pltpu.VMEM((2, TOP_K * TD * CHUNKS, LANES), F32),
                        pltpu.VMEM((TD, D_MODEL), F32),
                        pltpu.SemaphoreType.DMA,
                        pltpu.SemaphoreType.DMA((2,))],
        compiler_params=_cparams(("arbitrary",)),
        name="ffn_out",
    )(dest_tiles, dest_tiles, yb, gates_t, h_rows, y1, mod, post_g, wg_b, wu_b, wd_b)


def _sorted_rows_kernel(ids_ref, dest_ref, counts_ref, carry):
    p, i = pl.program_id(0), pl.program_id(1)
    tm = ids_ref.shape[1]
    ids = ids_ref[...]
    expert = lax.broadcasted_iota(I32, (N_EXPERTS, tm), 0)
    chose = [ids[j:j + 1, :] == expert for j in range(TOP_K)]
    onehot = sum(jnp.where(c, 1.0, 0.0) for c in chose)
    tile_counts = jnp.broadcast_to(onehot.sum(axis=1, keepdims=True), (N_EXPERTS, LANES))

    @pl.when((p == 0) & (i == 0))
    def _():
        carry[...] = jnp.zeros_like(carry)

    @pl.when(p == 0)
    def _():
        carry[...] += tile_counts

    @pl.when((p == 1) & (i == 0))
    def _():
        counts = carry[...]
        counts_ref[...] = counts
        row = lax.broadcasted_iota(I32, (N_EXPERTS, LANES), 0)
        incl = counts
        shift = 1
        while shift < N_EXPERTS:
            incl = incl + jnp.where(row >= shift, pltpu.roll(incl, shift, 0), 0.0)
            shift *= 2
        carry[...] = incl - counts

    @pl.when(p == 1)
    def _():
        earlier = (lax.broadcasted_iota(I32, (tm, tm), 0) < lax.broadcasted_iota(I32, (tm, tm), 1))
        before = jnp.dot(onehot.astype(BF16), jnp.where(earlier, 1.0, 0.0).astype(BF16),
                         preferred_element_type=F32)
        pos = carry[:, 0:1] + before
        rows = [jnp.where(c, pos, 0.0).sum(axis=0, keepdims=True) for c in chose]
        dest_ref[...] = jnp.concatenate(rows, axis=0).astype(I32)
        carry[...] += tile_counts


def _sorted_rows(ids):
    t = ids.shape[1]
    n_tiles = t // TM
    return pl.pallas_call(
        _sorted_rows_kernel,
        grid=(2, n_tiles),
        in_specs=[pl.BlockSpec((TOP_K, TM), lambda p, i: (0, i))],
        out_specs=[pl.BlockSpec((TOP_K, TM), lambda p, i: (0, i * p)),
                   pl.BlockSpec((N_EXPERTS, LANES), lambda p, i: (0, 0))],
        out_shape=[jax.ShapeDtypeStruct((TOP_K, t), I32),
                   jax.ShapeDtypeStruct((N_EXPERTS, LANES), F32)],
        scratch_shapes=[pltpu.VMEM((N_EXPERTS, LANES), F32)],
        compiler_params=_cparams(("arbitrary", "arbitrary")),
        name="sorted_rows",
    )(ids)


def _routing_tables(ids):
    t = ids.shape[1]
    n_blocks = t * TOP_K // TE
    n_visits = n_blocks + N_EXPERTS - 1
    dest, counts = _sorted_rows(ids)
    counts = counts[:, 0].astype(I32)
    end = jnp.cumsum(counts)
    start = end - counts

    first_blk = start // TE
    n_vis_e = jnp.where(counts > 0, (end - 1) // TE - first_blk + 1, 0)
    vend = jnp.cumsum(n_vis_e)
    vstart = vend - n_vis_e
    n_vis = vend[-1]
    v = jnp.arange(n_visits, dtype=I32)
    valid = v < n_vis
    mine = (vstart[None, :] <= v[:, None]) & (v[:, None] < vend[None, :])
    pick = lambda x: jnp.where(mine, x[None, :], 0).sum(axis=1)
    experts = jnp.arange(N_EXPERTS, dtype=I32)
    last_expert = jnp.max(jnp.where(n_vis_e > 0, experts, 0))
    ve = jnp.where(valid, pick(experts), last_expert).astype(I32)
    vb = jnp.where(valid, pick(first_blk) + v - pick(vstart), n_blocks - 1).astype(I32)
    lo = jnp.where(valid, jnp.clip(pick(start) - vb * TE, 0, TE), 0).astype(I32)
    hi = jnp.where(valid, jnp.clip(pick(end) - vb * TE, 0, TE), 0).astype(I32)
    vfirst = (valid & ((v == 0) | (vb != jnp.roll(vb, 1)))).astype(I32)
    lead = lambda x, first: jnp.concatenate([first, x])
    zero = jnp.zeros((1,), I32)
    vb, ve = lead(vb, vb[:1]), lead(ve, ve[:1])
    lo, hi, vfirst = lead(lo, zero), lead(hi, zero), lead(vfirst, zero)
    changed = (ve != jnp.roll(ve, 1)) & (jnp.arange(ve.shape[0]) > 0)
    vseq = jnp.cumsum(changed.astype(I32))
    return dest.astype(I32), (vb, ve, lo, hi, vfirst, vseq.astype(I32), n_vis.reshape(1).astype(I32))


def _rope_tables(n_tokens):
    rows = n_tokens // GRID_W
    row = jnp.repeat(jnp.arange(rows, dtype=F32), GRID_W)
    col = jnp.tile(jnp.arange(GRID_W, dtype=F32), rows)
    inv_freq = ROPE_THETA ** (-jnp.arange(0, AXIS_DIM, 2, dtype=F32) / AXIS_DIM)
    ang_r = row[:, None] * inv_freq
    ang_c = col[:, None] * inv_freq
    ang = jnp.concatenate([ang_r, ang_r, ang_c, ang_c], axis=-1)
    cos, sin = jnp.cos(ang), jnp.sin(ang)
    first = (jnp.arange(HEAD_DIM) % AXIS_DIM) < (AXIS_DIM // 2)
    return cos, jnp.where(first, -sin, 0.0), jnp.where(first, 0.0, sin)


def kernel(x_prompt, x_sample, cache_glob_k, cache_glob_v, cache_win_k, cache_win_v, c, c_ctx,
           w_ada, b_ada, attn_pre_g, attn_post_g, w_in, q_norm_g, k_norm_g, sink_logit, w_out,
           ffn_pre_g, ffn_post_g, w_router, router_bias, w_gate_e, w_up_e, w_down_e,
           w_gate_s, w_up_s, w_down_s):
    batch, seq, d = x_prompt.shape
    dec_batch, dec_seq, _ = x_sample.shape
    past = cache_glob_k.shape[2]
    tp, ts = batch * seq, dec_batch * dec_seq
    t = tp + ts
    l = 0

    xp = x_prompt.reshape(tp, d)
    xs = x_sample.reshape(ts, d)
    cond = jnp.concatenate([c_ctx[None, :], c, jnp.zeros((MOD_ROWS - 1 - dec_batch, d), F32)], axis=0)
    mod = _adaln(cond, w_ada[l], b_ada[l][None, :])

    w_in_b = w_in[l].astype(BF16)
    pre_g = attn_pre_g[l][None, :]
    qn, kn = q_norm_g[l][None, :], k_norm_g[l][None, :]
    cos, sa, sb = _rope_tables(dec_seq)
    sink = sink_logit[l]

    qg, kg, vg, qw, kw, vw = _qkv(xp, mod, pre_g, w_in_b, qn, kn, cos, sa, sb,
                                  rope=False, seq=seq, kv_dtype=F32)
    o_p = _ctx_attention(sink, qg, kg, vg, qw, kw, vw, seq=seq)
    new_kv = [a.reshape(batch, 1, seq, N_KV, HEAD_DIM) for a in (kg, vg, kw, vw)]

    qg, kg, vg, qw, kw, vw = _qkv(xs, mod, pre_g, w_in_b, qn, kn, cos, sa, sb,
                                  rope=True, seq=dec_seq, kv_dtype=BF16)
    caches = [a[:, l].reshape(dec_batch * past, KV_W)
              for a in (cache_glob_k, cache_glob_v, cache_win_k, cache_win_v)]
    o_s = _lat_attention(sink, qg, kg, vg, qw, kw, vw, *caches, seq=dec_seq, past=past)

    wr_t = w_router[l].T
    wr_hi = wr_t.astype(BF16)
    wr_lo = (wr_t - wr_hi.astype(F32)).astype(BF16)
    y1, h_rows, ids, gates = _post_attn(
        o_p, o_s, xp, xs, mod, w_out[l].astype(BF16), attn_post_g[l][None, :], ffn_pre_g[l][None, :],
        wr_hi, wr_lo, router_bias[l][:, None], seq=dec_seq)

    dest, visits = _routing_tables(ids)
    dest_tiles = (dest * CHUNKS).reshape(TOP_K, t // TD, TD).transpose(1, 0, 2).reshape(t // TD, 1, TOP_K * TD)
    xs_rows = _dispatch(dest_tiles, h_rows)
    yb = _experts(visits, xs_rows, w_gate_e[l], w_up_e[l], w_down_e[l])

    gates_t = gates.T
    shared = (w_gate_s[l].astype(BF16), w_up_s[l].astype(BF16), w_down_s[l].astype(BF16))
    post_g = ffn_post_g[l][None, :]
    y_p = _ffn_out(dest_tiles, yb, gates_t, h_rows, y1, mod, post_g, *shared,
                   tile0=0, n_tiles=tp // TD, tiles_per_row=0, mod_row0=0)
    y_s = _ffn_out(dest_tiles, yb, gates_t, h_rows, y1, mod, post_g, *shared,
                   tile0=tp // TD, n_tiles=ts // TD, tiles_per_row=dec_seq // TD, mod_row0=1)
    return (y_p.reshape(batch, seq, d), y_s.reshape(dec_batch, dec_seq, d), *new_kv)
```
